```python
import functools
import jax, jax.numpy as jnp
from jax import lax
import numpy as np

D_MODEL = 1024
BATCH = 4
SEQ = 8192
DEPTH = 4
DEC_BATCH = 32
DEC_SEQ = 16
PAST_LEN = 1024

CHUNK = 64
N_MIXERS = 2
N_FOX = (DEPTH + 1) // 2
N_POOL = DEPTH // 2
N_HEADS = 16
HEAD_DIM = D_MODEL // N_HEADS
ATTN_SCALE = HEAD_DIM ** -0.5
Q_BLOCK = 128
D_FF = 2816
PLE_DIM = 256
POOL_WINDOWS = (2, 4, 8, 16)
N_POOL_GROUPS = len(POOL_WINDOWS)
POOL_GROUP = D_MODEL // N_POOL_GROUPS
POOL_STATE = max(POOL_WINDOWS) - 1
ALPHA = (2.0 * DEPTH) ** 0.25
BETA = (8.0 * DEPTH) ** -0.25
LN_EPS = 1e-5

kernel_name = "fox_pool_macaron_deepnorm_stream_step"


def _layer_norm(x, g, b):
    xf = x.astype(jnp.float32)
    mu = jnp.mean(xf, axis=-1, keepdims=True)
    var = jnp.mean(jnp.square(xf - mu), axis=-1, keepdims=True)
    return ((xf - mu) * lax.rsqrt(var + LN_EPS) * g + b).astype(x.dtype)


def _post_norm(x, sub, g, b):
    return _layer_norm(ALPHA * x + sub, g, b)


def _swiglu(x, w_in, w_out):
    h = x @ w_in
    a, u = h[..., :D_FF], h[..., D_FF:]
    return (jax.nn.silu(a) * u) @ w_out


def _fox_project(x, w_in, b_f):
    B, T, _ = x.shape
    h = x @ w_in
    qkv = h[..., :3 * D_MODEL].reshape(B, T, 3, N_HEADS, HEAD_DIM)
    logf = jax.nn.log_sigmoid((h[..., 3 * D_MODEL:] + b_f).astype(jnp.float32))
    return qkv[:, :, 0], qkv[:, :, 1], qkv[:, :, 2], logf


def _fox_attend(q, f_q, pos_q, k, v, f_k, pos_k):
    s = jnp.einsum("bqhd,bkhd->bhqk", q, k).astype(jnp.float32) * ATTN_SCALE
    s = s + (f_q[..., :, None] - f_k[..., None, :])
    s = jnp.where(pos_k[None, :] <= pos_q[:, None], s, -jnp.inf)
    w = jax.nn.softmax(s, axis=-1).astype(v.dtype)
    return jnp.einsum("bhqk,bkhd->bqhd", w, v)


def _fox_prompt(x, w_in, b_f, w_o):
    B, S, _ = x.shape
    q, k, v, logf = _fox_project(x, w_in, b_f)
    F = jnp.cumsum(logf, axis=1).transpose(0, 2, 1)
    pos = jnp.arange(S)
    nb = S // Q_BLOCK
    q_blocks = q.reshape(B, nb, Q_BLOCK, N_HEADS, HEAD_DIM).transpose(1, 0, 2, 3, 4)
    f_blocks = F.reshape(B, N_HEADS, nb, Q_BLOCK).transpose(2, 0, 1, 3)
    p_blocks = pos.reshape(nb, Q_BLOCK)

    def one_block(args):
        qb, fb, pb = args
        return _fox_attend(qb, fb, pb, k, v, F, pos)

    o = lax.map(one_block, (q_blocks, f_blocks, p_blocks))
    o = o.transpose(1, 0, 2, 3, 4).reshape(B, S, D_MODEL)
    return o @ w_o, (k, v, logf)


def _fox_sample(x, cache_k, cache_v, cache_logf, w_in, b_f, w_o):
    B, T, _ = x.shape
    P = cache_k.shape[1]
    q, k, v, logf = _fox_project(x, w_in, b_f)
    logf_all = jnp.concatenate([cache_logf.astype(jnp.float32), logf], axis=1)
    F = jnp.cumsum(logf_all, axis=1).transpose(0, 2, 1)
    k_all = jnp.concatenate([cache_k, k.astype(cache_k.dtype)], axis=1)
    v_all = jnp.concatenate([cache_v, v.astype(cache_v.dtype)], axis=1)
    pos = jnp.arange(P + T)
    o = _fox_attend(q, F[:, :, P:], pos[P:], k_all, v_all, F, pos)
    return o.reshape(B, T, D_MODEL) @ w_o, (k, v, logf)


def _pool_mix(x_ext, n_hist, w_pool, scale):
    B, L, _ = x_ext.shape
    T = L - n_hist
    xf = x_ext.astype(jnp.float32)
    cs = jnp.concatenate([jnp.zeros((B, 1, D_MODEL), jnp.float32), jnp.cumsum(xf, axis=1)], axis=1)
    t = jnp.arange(n_hist, L)
    hi = cs[:, n_hist + 1:]
    groups = []
    for g, w in enumerate(POOL_WINDOWS):
        sl = slice(g * POOL_GROUP, (g + 1) * POOL_GROUP)
        lo_idx = jnp.maximum(t + 1 - w, 0)
        lo = jnp.take(cs[..., sl], lo_idx, axis=1)
        cnt = (t + 1 - lo_idx).astype(jnp.float32)
        groups.append((hi[..., sl] - lo) / cnt[None, :, None])
    pooled = jnp.concatenate(groups, axis=-1) - xf[:, n_hist:]
    pooled = pooled.reshape(B, T, N_POOL_GROUPS, POOL_GROUP).astype(x_ext.dtype)
    y = jnp.einsum("btgc,gcd->btgd", pooled, w_pool).reshape(B, T, D_MODEL)
    return y * scale


def _pool_prompt(x, w_pool, scale):
    return _pool_mix(x, 0, w_pool, scale), (x[:, -POOL_STATE:],)


def _pool_sample(x, state, w_pool, scale):
    ext = jnp.concatenate([state, x.astype(state.dtype)], axis=1)
    return _pool_mix(ext, POOL_STATE, w_pool, scale), (ext[:, -POOL_STATE:],)


def _layer(x, p_i, i, mixer, ln_g, ln_b, ffn_w_in, ffn_w_out, ple_w_proj, ple_w_gate, ple_b_gate):
    x = _post_norm(x, 0.5 * _swiglu(x, ffn_w_in[i, 0], ffn_w_out[i, 0]), ln_g[i, 0], ln_b[i, 0])
    m, state = mixer(x)
    x = _post_norm(x, m, ln_g[i, 1], ln_b[i, 1])
    x = _post_norm(x, 0.5 * _swiglu(x, ffn_w_in[i, 1], ffn_w_out[i, 1]), ln_g[i, 2], ln_b[i, 2])
    gate = jax.nn.sigmoid(x @ ple_w_gate[i] + ple_b_gate[i])
    x = _post_norm(x, (p_i @ ple_w_proj[i]) * gate, ln_g[i, 3], ln_b[i, 3])
    return x, state


def setup_inputs(seed: int = 0) -> dict:
    key = jax.random.key(seed)
    ks = jax.random.split(key, 24)
    f32 = jnp.float32
    nrm = lambda k, shape, s: jax.random.normal(k, shape, f32) * s
    b_f_base = jnp.linspace(1.0, 5.0, N_HEADS, dtype=f32)
    return {
        "x_prompt": nrm(ks[0], (BATCH, SEQ, D_MODEL), 1.0),
        "x_sample": nrm(ks[1], (DEC_BATCH, DEC_SEQ, D_MODEL), 1.0),
        "cache_fox_k": nrm(ks[2], (N_FOX, DEC_BATCH, PAST_LEN, N_HEADS, HEAD_DIM), 1.0),
        "cache_fox_v": nrm(ks[3], (N_FOX, DEC_BATCH, PAST_LEN, N_HEADS, HEAD_DIM), 1.0),
        "cache_fox_logf": jax.nn.log_sigmoid(3.0 + nrm(ks[4], (N_FOX, DEC_BATCH, PAST_LEN, N_HEADS), 1.0)),
        "state_pool": nrm(ks[5], (N_POOL, DEC_BATCH, POOL_STATE, D_MODEL), 1.0),
        "p_prompt": nrm(ks[6], (DEPTH, BATCH, SEQ, PLE_DIM), 1.0),
        "p_sample": nrm(ks[7], (DEPTH, DEC_BATCH, DEC_SEQ, PLE_DIM), 1.0),
        "ln_g": 1.0 + nrm(ks[8], (DEPTH, 4, D_MODEL), 0.02),
        "ln_b": nrm(ks[9], (DEPTH, 4, D_MODEL), 0.02),
        "ffn_w_in": nrm(ks[10], (DEPTH, 2, D_MODEL, 2 * D_FF), D_MODEL ** -0.5),
        "ffn_w_out": nrm(ks[11], (DEPTH, 2, D_FF, D_MODEL), BETA * D_FF ** -0.5),
        "fox_w_in": jnp.concatenate([
            nrm(ks[12], (N_FOX, D_MODEL, 3 * D_MODEL), D_MODEL ** -0.5),
            nrm(ks[13], (N_FOX, D_MODEL, N_HEADS), 0.5 * D_MODEL ** -0.5)], axis=-1),
        "fox_b_f": b_f_base[None, :] + nrm(ks[14], (N_FOX, N_HEADS), 0.1),
        "fox_w_o": nrm(ks[15], (N_FOX, D_MODEL, D_MODEL), BETA * D_MODEL ** -0.5),
        "pool_w": nrm(ks[16], (N_POOL, N_POOL_GROUPS, POOL_GROUP, POOL_GROUP), BETA * POOL_GROUP ** -0.5),
        "pool_scale": 1.0 + nrm(ks[17], (N_POOL, D_MODEL), 0.02),
        "ple_w_proj": nrm(ks[18], (DEPTH, PLE_DIM, D_MODEL), BETA * PLE_DIM ** -0.5),
        "ple_w_gate": nrm(ks[19], (DEPTH, D_MODEL, D_MODEL), D_MODEL ** -0.5),
        "ple_b_gate": nrm(ks[20], (DEPTH, D_MODEL), 0.02),
    }


def reference(x_prompt, x_sample, cache_fox_k, cache_fox_v, cache_fox_logf, state_pool,
              p_prompt, p_sample, ln_g, ln_b, ffn_w_in, ffn_w_out, fox_w_in, fox_b_f, fox_w_o,
              pool_w, pool_scale, ple_w_proj, ple_w_gate, ple_b_gate):
    shared = dict(ln_g=ln_g, ln_b=ln_b, ffn_w_in=ffn_w_in, ffn_w_out=ffn_w_out,
                  ple_w_proj=ple_w_proj, ple_w_gate=ple_w_gate, ple_b_gate=ple_b_gate)
    yp, ys = x_prompt, x_sample
    kp, vp, fp, poolp = [], [], [], []
    ksm, vsm, fsm, pools = [], [], [], []
    for i in range(DEPTH):
        j = i // N_MIXERS
        if i % N_MIXERS == 0:
            mix_p = functools.partial(_fox_prompt, w_in=fox_w_in[j], b_f=fox_b_f[j], w_o=fox_w_o[j])
            mix_s = functools.partial(_fox_sample, cache_k=cache_fox_k[j], cache_v=cache_fox_v[j],
                                      cache_logf=cache_fox_logf[j], w_in=fox_w_in[j],
                                      b_f=fox_b_f[j], w_o=fox_w_o[j])
            yp, (k1, v1, f1) = _layer(yp, p_prompt[i], i, mix_p, **shared)
            ys, (k2, v2, f2) = _layer(ys, p_sample[i], i, mix_s, **shared)
            kp.append(k1); vp.append(v1); fp.append(f1)
            ksm.append(k2); vsm.append(v2); fsm.append(f2)
        else:
            mix_p = functools.partial(_pool_prompt, w_pool=pool_w[j], scale=pool_scale[j])
            mix_s = functools.partial(_pool_sample, state=state_pool[j], w_pool=pool_w[j], scale=pool_scale[j])
            yp, (s1,) = _layer(yp, p_prompt[i], i, mix_p, **shared)
            ys, (s2,) = _layer(ys, p_sample[i], i, mix_s, **shared)
            poolp.append(s1); pools.append(s2)
    return (yp, ys, jnp.stack(kp), jnp.stack(vp), jnp.stack(fp), jnp.stack(poolp),
            jnp.stack(ksm), jnp.stack(vsm), jnp.stack(fsm), jnp.stack(pools))
```

```python
import functools

import numpy as np
import jax
import jax.numpy as jnp
from jax import lax
from jax.experimental import pallas as pl
from jax.experimental.pallas import tpu as pltpu

F32 = jnp.float32
BF16 = jnp.bfloat16

LANES = 128
HEAD_DIM = 64
POOL_WINDOWS = (2, 4, 8, 16)
POOL_HALO = 16
LN_EPS = 1e-5
LOG2E = 1.4426950408889634
NEG_BIG = -1e30
VMEM_LIMIT = 56 * 1024 * 1024

ROW_TILE = 512
FFN_CHUNK = 256
ATTN_TILE = 512


def _params(n_axes=1):
    return pltpu.CompilerParams(
        dimension_semantics=("arbitrary",) * n_axes, vmem_limit_bytes=VMEM_LIMIT)


def _resident(shape):
    nd = len(shape)
    return pl.BlockSpec(shape, lambda *_: (0,) * nd, pipeline_mode=pl.Buffered(1))


def _rows(tm, width):
    return pl.BlockSpec((tm, width), lambda i: (i, 0))


def _sigmoid(z):
    return 1.0 / (1.0 + jnp.exp(-z))


def _post_norm(x, sub, g, b, alpha):
    y = alpha * x + sub
    mu = jnp.mean(y, axis=-1, keepdims=True)
    yc = y - mu
    var = jnp.mean(yc * yc, axis=-1, keepdims=True)
    return yc * lax.rsqrt(var + LN_EPS) * g + b


def _split3(x):
    hi = x.astype(BF16)
    r1 = x - hi.astype(F32)
    mid = r1.astype(BF16)
    lo = (r1 - mid.astype(F32)).astype(BF16)
    return hi, mid, lo


def _dot(a, b):
    return jnp.dot(a, b, preferred_element_type=F32)


def _dot_nt(a, b):
    return lax.dot_general(a, b, (((1,), (1,)), ((), ())), preferred_element_type=F32)


def _ffn_ln_kernel(x_ref, win_ref, wout_ref, g_ref, b_ref, o_ref, xb_sc, acc_sc, *, alpha):
    x = x_ref[...]
    xb_sc[...] = x.astype(BF16)
    acc_sc[...] = jnp.zeros_like(acc_sc)

    def chunk(j, carry):
        xb = xb_sc[...]
        a = _dot(xb, win_ref[0, j])
        u = _dot(xb, win_ref[1, j])
        act = (a * _sigmoid(a) * u).astype(BF16)
        acc_sc[...] += _dot(act, wout_ref[j])
        return carry

    lax.fori_loop(0, win_ref.shape[1], chunk, 0)
    o_ref[...] = _post_norm(x, 0.5 * acc_sc[...], g_ref[...], b_ref[...], alpha)


def _ffn_ln(x, w_in, w_out, g, b, alpha):
    rows, d = x.shape
    tm = min(ROW_TILE, rows)
    return pl.pallas_call(
        functools.partial(_ffn_ln_kernel, alpha=alpha),
        grid=(rows // tm,),
        in_specs=[_rows(tm, d), _resident(w_in.shape), _resident(w_out.shape),
                  _resident(g.shape), _resident(b.shape)],
        out_specs=_rows(tm, d),
        out_shape=jax.ShapeDtypeStruct((rows, d), F32),
        scratch_shapes=[pltpu.VMEM((tm, d), BF16), pltpu.VMEM((tm, d), F32)],
        compiler_params=_params(),
        name="ffn_ln",
    )(x, w_in, w_out, g, b)


def _ple_ln_kernel(x_ref, p_ref, wg_ref, bg_ref, wp_ref, g_ref, b_ref, o_ref, *, alpha):
    x = x_ref[...]
    gate = _sigmoid(_dot(x.astype(BF16), wg_ref[...]) + bg_ref[...])
    proj = _dot(p_ref[...].astype(BF16), wp_ref[...])
    o_ref[...] = _post_norm(x, proj * gate, g_ref[...], b_ref[...], alpha)


def _ple_ln(x, p, wg, bg, wp, g, b, alpha):
    rows, d = x.shape
    tm = min(ROW_TILE, rows)
    return pl.pallas_call(
        functools.partial(_ple_ln_kernel, alpha=alpha),
        grid=(rows // tm,),
        in_specs=[_rows(tm, d), _rows(tm, p.shape[1]), _resident(wg.shape), _resident(bg.shape),
                  _resident(wp.shape), _resident(g.shape), _resident(b.shape)],
        out_specs=_rows(tm, d),
        out_shape=jax.ShapeDtypeStruct((rows, d), F32),
        compiler_params=_params(),
        name="ple_ln",
    )(x, p, wg, bg, wp, g, b)


def _proj_ln_kernel(x_ref, o_ref_in, wo_ref, g_ref, b_ref, out_ref, *, alpha):
    m = _dot(o_ref_in[...], wo_ref[...])
    out_ref[...] = _post_norm(x_ref[...], m, g_ref[...], b_ref[...], alpha)


def _proj_ln(x, o, wo, g, b, alpha):
    rows, d = x.shape
    tm = min(ROW_TILE, rows)
    return pl.pallas_call(
        functools.partial(_proj_ln_kernel, alpha=alpha),
        grid=(rows // tm,),
        in_specs=[_rows(tm, d), _rows(tm, d), _resident(wo.shape), _resident(g.shape),
                  _resident(b.shape)],
        out_specs=_rows(tm, d),
        out_shape=jax.ShapeDtypeStruct((rows, d), F32),
        compiler_params=_params(),
        name="attn_out_ln",
    )(x, o, wo, g, b)


def _pool_ln_kernel(x_ref, halo_ref, w_ref, scale_ref, g_ref, b_ref, o_ref, ext_sc,
                    *, alpha, seq_len, tm):
    i = pl.program_id(0)
    start = (i * tm) % seq_len
    x = x_ref[...]
    d = x.shape[1]
    grp = d // len(POOL_WINDOWS)
    ext_sc[0:POOL_HALO, :] = jnp.where(start == 0, 0.0, halo_ref[...])
    ext_sc[POOL_HALO:, :] = x
    pos = (start + lax.broadcasted_iota(jnp.int32, (tm, 1), 0)) % seq_len
    outs = []
    for gi, w in enumerate(POOL_WINDOWS):
        sl = slice(gi * grp, (gi + 1) * grp)
        tot = x[:, sl]
        for back in range(1, w):
            tot = tot + ext_sc[POOL_HALO - back:POOL_HALO - back + tm, sl]
        cnt = jnp.minimum(pos + 1, w).astype(F32)
        pooled = tot / cnt - x[:, sl]
        outs.append(_dot(pooled.astype(BF16), w_ref[gi]))
    y = jnp.concatenate(outs, axis=1) * scale_ref[...]
    o_ref[...] = _post_norm(x, y, g_ref[...], b_ref[...], alpha)


def _pool_ln(x, w_pool, scale, g, b, alpha, seq_len):
    rows, d = x.shape
    tm = min(ROW_TILE, rows)
    assert seq_len % tm == 0 or tm % seq_len == 0
    halo_blocks = tm // POOL_HALO
    return pl.pallas_call(
        functools.partial(_pool_ln_kernel, alpha=alpha, seq_len=seq_len, tm=tm),
        grid=(rows // tm,),
        in_specs=[_rows(tm, d),
                  pl.BlockSpec((POOL_HALO, d), lambda i: (jnp.maximum(i * halo_blocks - 1, 0), 0)),
                  _resident(w_pool.shape), _resident(scale.shape), _resident(g.shape),
                  _resident(b.shape)],
        out_specs=_rows(tm, d),
        out_shape=jax.ShapeDtypeStruct((rows, d), F32),
        scratch_shapes=[pltpu.VMEM((POOL_HALO + tm, d), F32)],
        compiler_params=_params(),
        name="pool_ln",
    )(x, x, w_pool, scale, g, b)


def _log_sigmoid(z):
    return jnp.minimum(z, 0.0) - jnp.log1p(jnp.exp(-jnp.abs(z)))


def _lane_group_select(a, b, c):
    lane = lax.broadcasted_iota(jnp.int32, a.shape, 1)
    a, b, c = a.astype(F32), b.astype(F32), c.astype(F32)
    sel = jnp.where(lane < 16, a, jnp.where(lane < 32, b, jnp.where(
        lane < 48, c, jnp.where(lane < 64, 1.0, 0.0))))
    return sel.astype(BF16)


def _expand_heads(t):
    n = t.shape[1] // LANES
    return jnp.concatenate(
        [t[:, (gidx // 2) * LANES:(gidx // 2 + 1) * LANES] for gidx in range(2 * n)], axis=1)


def _fox_proj_prompt_kernel(x_ref, wq_ref, wk_ref, wv_ref, wf_ref, bf_ref, tri_ref, pq_ref, pk_ref,
                            qx_ref, kx_ref, vb_ref, k_ref, v_ref, logf_ref, carry_sc,
                            *, tiles_per_seq, q_scale, n_heads):
    i = pl.program_id(0)

    @pl.when(i % tiles_per_seq == 0)
    def _():
        carry_sc[...] = jnp.zeros_like(carry_sc)

    xb = x_ref[...].astype(BF16)
    q = _dot(xb, wq_ref[...]) * q_scale
    k = _dot(xb, wk_ref[...])
    v = _dot(xb, wv_ref[...])
    k_ref[...] = k
    v_ref[...] = v
    vb_ref[...] = v.astype(BF16)

    logf = _log_sigmoid(_dot(xb, wf_ref[...]) + bf_ref[...])
    logf_ref[...] = logf[:, :n_heads]

    hi, mid, lo = _split3(logf)
    cs = _dot(tri_ref[...], jnp.concatenate([hi, mid, lo], axis=1))
    cum = (cs[:, :LANES] + cs[:, LANES:2 * LANES] + cs[:, 2 * LANES:]) + carry_sc[...]
    tm = cum.shape[0]
    carry_sc[...] = cum[tm - 1:tm, :]

    fh, fm, fl = _split3(cum * LOG2E)
    aug = _lane_group_select(fh, fm, fl)
    aq = _dot(aug, pq_ref[...])
    ak = _dot(aug, pk_ref[...])
    lane = lax.broadcasted_iota(jnp.int32, aq.shape, 1)
    own = (((lane >> 6) ^ (lane >> 7)) & 1) == 0
    qx_ref[...] = jnp.where(own, _expand_heads(q), aq).astype(BF16)
    kx_ref[...] = jnp.where(own, _expand_heads(k), ak).astype(BF16)


def _aug_placement(n_heads):
    pq = np.zeros((LANES, 2 * n_heads * HEAD_DIM), np.float32)
    pk = np.zeros_like(pq)
    for h in range(n_heads):
        base = h * LANES + (HEAD_DIM if h % 2 == 0 else 0)
        for piece in range(3):
            pq[16 * piece + h, base + piece] = 1.0
            pq[48 + h, base + 3 + piece] = 1.0
            pk[48 + h, base + piece] = 1.0
            pk[16 * piece + h, base + 3 + piece] = -1.0
    return jnp.asarray(pq, BF16), jnp.asarray(pk, BF16)


def _fox_proj_prompt(x, wq, wk, wv, wf, bf, seq_len, q_scale, n_heads):
    rows, d = x.shape
    tm = min(ROW_TILE, seq_len)
    tri = jnp.asarray(np.tril(np.ones((tm, tm), np.float32)), BF16)
    pq, pk = _aug_placement(n_heads)
    wide = pl.BlockSpec((tm, 2 * d), lambda i: (i, 0))
    return pl.pallas_call(
        functools.partial(_fox_proj_prompt_kernel, tiles_per_seq=seq_len // tm, q_scale=q_scale,
                          n_heads=n_heads),
        grid=(rows // tm,),
        in_specs=[_rows(tm, d), _resident(wq.shape), _resident(wk.shape), _resident(wv.shape),
                  _resident(wf.shape), _resident(bf.shape), _resident(tri.shape),
                  _resident(pq.shape), _resident(pk.shape)],
        out_specs=[wide, wide, _rows(tm, d), _rows(tm, d), _rows(tm, d), _rows(tm, n_heads)],
        out_shape=[jax.ShapeDtypeStruct((rows, 2 * d), BF16), jax.ShapeDtypeStruct((rows, 2 * d), BF16),
                   jax.ShapeDtypeStruct((rows, d), BF16), jax.ShapeDtypeStruct((rows, d), F32),
                   jax.ShapeDtypeStruct((rows, d), F32), jax.ShapeDtypeStruct((rows, n_heads), F32)],
        scratch_shapes=[pltpu.VMEM((1, LANES), F32)],
        compiler_params=_params(),
        name="fox_proj_prompt",
    )(x, wq, wk, wv, wf, bf, tri, pq, pk)


def _fox_proj_sample_kernel(x_ref, wq_ref, wk_ref, wv_ref, wf_ref, bf_ref,
                            q_ref, k_ref, v_ref, logf_ref, *, q_scale, n_heads):
    xb = x_ref[...].astype(BF16)
    q_ref[...] = (_dot(xb, wq_ref[...]) * q_scale).astype(BF16)
    k_ref[...] = _dot(xb, wk_ref[...])
    v_ref[...] = _dot(xb, wv_ref[...])
    logf = _log_sigmoid(_dot(xb, wf_ref[...]) + bf_ref[...])
    logf_ref[...] = logf[:, :n_heads]


def _fox_proj_sample(x, wq, wk, wv, wf, bf, q_scale, n_heads):
    rows, d = x.shape
    tm = min(ROW_TILE, rows)
    return pl.pallas_call(
        functools.partial(_fox_proj_sample_kernel, q_scale=q_scale, n_heads=n_heads),
        grid=(rows // tm,),
        in_specs=[_rows(tm, d), _resident(wq.shape), _resident(wk.shape), _resident(wv.shape),
                  _resident(wf.shape), _resident(bf.shape)],
        out_specs=[_rows(tm, d), _rows(tm, d), _rows(tm, d), _rows(tm, n_heads)],
        out_shape=[jax.ShapeDtypeStruct((rows, d), BF16), jax.ShapeDtypeStruct((rows, d), F32),
                   jax.ShapeDtypeStruct((rows, d), F32), jax.ShapeDtypeStruct((rows, n_heads), F32)],
        compiler_params=_params(),
        name="fox_proj_sample",
    )(x, wq, wk, wv, wf, bf)


def _fox_attn_prompt_kernel(qx_ref, kx_ref, v_ref, o_ref, m_sc, l_sc, acc_sc, *, tile):
    seq = qx_ref.shape[0]
    row = lax.broadcasted_iota(jnp.int32, (tile, tile), 0)
    col = lax.broadcasted_iota(jnp.int32, (tile, tile), 1)
    causal = col <= row
    low_half = lax.broadcasted_iota(jnp.int32, (tile, LANES), 1) < HEAD_DIM

    def q_block(qi, carry):
        q0 = pl.multiple_of(qi * tile, tile)
        for h in range(2):
            m_sc[h] = jnp.full(m_sc.shape[1:], NEG_BIG, F32)
            l_sc[h] = jnp.zeros(l_sc.shape[1:], F32)
            acc_sc[h] = jnp.zeros(acc_sc.shape[1:], F32)

        def kv_block(k0, masked):
            vb = v_ref[pl.ds(k0, tile), :]
            for h in range(2):
                qa = qx_ref[pl.ds(q0, tile), h * LANES:(h + 1) * LANES]
                ka = kx_ref[pl.ds(k0, tile), h * LANES:(h + 1) * LANES]
                s = _dot_nt(qa, ka)
                if masked:
                    s = jnp.where(causal, s, -jnp.inf)
                m_prev = m_sc[h]
                m_new = jnp.maximum(m_prev, jnp.max(s, axis=1, keepdims=True))
                alpha = jnp.exp2(m_prev - m_new)
                p = jnp.exp2(s - m_new)
                l_sc[h] = alpha * l_sc[h] + jnp.sum(p, axis=1, keepdims=True)
                acc_sc[h] = alpha * acc_sc[h] + _dot(p.astype(BF16), vb)
                m_sc[h] = m_new

        def full_block(kj, c):
            kv_block(pl.multiple_of(kj * tile, tile), False)
            return c

        lax.fori_loop(0, qi, full_block, 0)
        kv_block(q0, True)
        o0 = acc_sc[0] / l_sc[0]
        o1 = acc_sc[1] / l_sc[1]
        o_ref[pl.ds(q0, tile), :] = jnp.where(low_half, o0, o1).astype(o_ref.dtype)
        return carry

    lax.fori_loop(0, seq // tile, q_block, 0)


def _fox_attn_prompt(qx, kx, vb, batch, seq_len):
    rows, d = vb.shape
    pairs = d // LANES
    tile = min(ATTN_TILE, seq_len)
    return pl.pallas_call(
        functools.partial(_fox_attn_prompt_kernel, tile=tile),
        grid=(batch, pairs),
        in_specs=[pl.BlockSpec((seq_len, 2 * LANES), lambda b, p: (b, p)),
                  pl.BlockSpec((seq_len, 2 * LANES), lambda b, p: (b, p)),
                  pl.BlockSpec((seq_len, LANES), lambda b, p: (b, p))],
        out_specs=pl.BlockSpec((seq_len, LANES), lambda b, p: (b, p)),
        out_shape=jax.ShapeDtypeStruct((rows, d), BF16),
        scratch_shapes=[pltpu.VMEM((2, tile, 1), F32), pltpu.VMEM((2, tile, 1), F32),
                        pltpu.VMEM((2, tile, LANES), F32)],
        compiler_params=_params(2),
        name="fox_attn_prompt",
    )(qx, kx, vb)


def _fox_attn_sample_kernel(q_ref, kn_ref, vn_ref, ck_ref, cv_ref, lft_ref, o_ref, kall_sc, vall_sc,
                            *, n_heads):
    t_new, d = q_ref.shape[1], q_ref.shape[2]
    past = ck_ref.shape[1]
    keys = kall_sc.shape[0]
    qrows = n_heads * t_new

    kall_sc[0:past, :] = ck_ref[0].astype(BF16)
    vall_sc[0:past, :] = cv_ref[0].astype(BF16)
    pad = jnp.zeros((keys - past - t_new, d), BF16)
    kall_sc[past:, :] = jnp.concatenate([kn_ref[0].astype(BF16), pad], axis=0)
    vall_sc[past:, :] = jnp.concatenate([vn_ref[0].astype(BF16), pad], axis=0)

    cum = lft_ref[0]
    key_idx = lax.broadcasted_iota(jnp.int32, cum.shape, 1)
    step = 1
    while step < keys:
        cum = cum + jnp.where(key_idx >= step, pltpu.roll(cum, step, axis=1), 0.0)
        step *= 2

    r_head = lax.broadcasted_iota(jnp.int32, (qrows, n_heads), 0) // t_new
    rep_head = jnp.where(r_head == lax.broadcasted_iota(jnp.int32, (qrows, n_heads), 1),
                         1.0, 0.0).astype(BF16)
    r_query = lax.broadcasted_iota(jnp.int32, (qrows, t_new), 0) % t_new
    rep_query = jnp.where(r_query == lax.broadcasted_iota(jnp.int32, (qrows, t_new), 1),
                          1.0, 0.0).astype(BF16)

    c_hi, c_mid, c_lo = _split3(cum * LOG2E)
    cum_rows = _dot(rep_head, c_hi) + _dot(rep_head, c_mid) + _dot(rep_head, c_lo)
    kidx = lax.broadcasted_iota(jnp.int32, (qrows, keys), 1)
    qpos = past + lax.broadcasted_iota(jnp.int32, (qrows, keys), 0) % t_new
    cum_q = jnp.sum(jnp.where(kidx == qpos, cum_rows, 0.0), axis=1, keepdims=True)

    lane_head = lax.broadcasted_iota(jnp.int32, (qrows, d), 1) // HEAD_DIM
    row_head = lax.broadcasted_iota(jnp.int32, (qrows, d), 0) // t_new
    q_rep = _dot(rep_query, q_ref[0])
    q_exp = jnp.where(lane_head == row_head, q_rep, 0.0).astype(BF16)

    s = _dot_nt(q_exp, kall_sc[...]) + (cum_q - cum_rows)
    s = jnp.where(kidx <= qpos, s, -jnp.inf)
    m = jnp.max(s, axis=1, keepdims=True)
    p = jnp.exp2(s - m)
    l = jnp.sum(p, axis=1, keepdims=True)
    o_all = _dot(p.astype(BF16), vall_sc[...]) / l
    o_all = jnp.where(lane_head == row_head, o_all, 0.0)
    out = o_all[0:t_new, :]
    for h in range(1, n_heads):
        out = out + o_all[h * t_new:(h + 1) * t_new, :]
    o_ref[0] = out.astype(o_ref.dtype)


def _fox_attn_sample(q, k_new, v_new, cache_k, cache_v, lft, n_heads):
    nb, t_new, d = q.shape
    past = cache_k.shape[1]
    keys = lft.shape[2]
    per_stream = lambda shape: pl.BlockSpec((1,) + shape, lambda b: (b, 0, 0))
    return pl.pallas_call(
        functools.partial(_fox_attn_sample_kernel, n_heads=n_heads),
        grid=(nb,),
        in_specs=[per_stream((t_new, d)), per_stream((t_new, d)), per_stream((t_new, d)),
                  per_stream((past, d)), per_stream((past, d)), per_stream((n_heads, keys))],
        out_specs=per_stream((t_new, d)),
        out_shape=jax.ShapeDtypeStruct((nb, t_new, d), BF16),
        scratch_shapes=[pltpu.VMEM((keys, d), BF16), pltpu.VMEM((keys, d), BF16)],
        compiler_params=_params(),
        name="fox_attn_sample",
    )(q, k_new, v_new, cache_k, cache_v, lft)


def kernel(x_prompt, x_sample, cache_fox_k, cache_fox_v, cache_fox_logf, state_pool, p_prompt, p_sample,
           ln_g, ln_b, ffn_w_in, ffn_w_out, fox_w_in, fox_b_f, fox_w_o, pool_w, pool_scale,
           ple_w_proj, ple_w_gate, ple_b_gate):
    batch, seq, d = x_prompt.shape
    nb, t_new, _ = x_sample.shape
    depth = ln_g.shape[0]
    n_heads = d // HEAD_DIM
    d_ff = ffn_w_out.shape[2]
    past = cache_fox_k.shape[2]
    n_state = state_pool.shape[2]
    alpha = (2.0 * depth) ** 0.25
    q_scale = HEAD_DIM ** -0.5 * LOG2E
    n_chunks = d_ff // FFN_CHUNK

    xp = x_prompt.reshape(batch * seq, d)
    xs = x_sample.reshape(nb * t_new, d)
    row = lambda a: a.reshape(1, -1)

    kp, vp, fp, poolp, ksm, vsm, fsm, pools = [], [], [], [], [], [], [], []
    for i in range(depth):
        j = i // 2

        def ffn(x, s):
            w_in = ffn_w_in[i, s].astype(BF16).reshape(d, 2, n_chunks, FFN_CHUNK).transpose(1, 2, 0, 3)
            w_out = ffn_w_out[i, s].astype(BF16).reshape(n_chunks, FFN_CHUNK, d)
            return _ffn_ln(x, w_in, w_out, row(ln_g[i, 2 * s]), row(ln_b[i, 2 * s]), alpha)

        xp = ffn(xp, 0)
        xs = ffn(xs, 0)

        if i % 2 == 0:
            w = fox_w_in[j]
            wq, wk, wv = (w[:, c * d:(c + 1) * d].astype(BF16) for c in range(3))
            reps = LANES // n_heads
            wf = jnp.tile(w[:, 3 * d:], (1, reps)).astype(BF16)
            bf = jnp.tile(fox_b_f[j], reps).reshape(1, LANES)
            wo = fox_w_o[j].astype(BF16)
            g, b = row(ln_g[i, 1]), row(ln_b[i, 1])

            qx, kx, vb, k1, v1, f1 = _fox_proj_prompt(xp, wq, wk, wv, wf, bf, seq, q_scale, n_heads)
            o = _fox_attn_prompt(qx, kx, vb, batch, seq)
            xp = _proj_ln(xp, o, wo, g, b, alpha)
            kp.append(k1.reshape(batch, seq, n_heads, HEAD_DIM))
            vp.append(v1.reshape(batch, seq, n_heads, HEAD_DIM))
            fp.append(f1.reshape(batch, seq, n_heads))

            q2, k2, v2, f2 = _fox_proj_sample(xs, wq, wk, wv, wf, bf, q_scale, n_heads)
            keys = -(-(past + t_new) // LANES) * LANES
            lft = jnp.concatenate(
                [cache_fox_logf[j].astype(F32), f2.reshape(nb, t_new, n_heads),
                 jnp.zeros((nb, keys - past - t_new, n_heads), F32)], axis=1).transpose(0, 2, 1)
            o2 = _fox_attn_sample(q2.reshape(nb, t_new, d), k2.reshape(nb, t_new, d),
                                  v2.reshape(nb, t_new, d), cache_fox_k[j].reshape(nb, past, d),
                                  cache_fox_v[j].reshape(nb, past, d), lft, n_heads)
            xs = _proj_ln(xs, o2.reshape(nb * t_new, d), wo, g, b, alpha)
            ksm.append(k2.reshape(nb, t_new, n_heads, HEAD_DIM))
            vsm.append(v2.reshape(nb, t_new, n_heads, HEAD_DIM))
            fsm.append(f2.reshape(nb, t_new, n_heads))
        else:
            wpool = pool_w[j].astype(BF16)
            sc = row(pool_scale[j])
            g, b = row(ln_g[i, 1]), row(ln_b[i, 1])
            poolp.append(xp.reshape(batch, seq, d)[:, seq - n_state:])
            xp = _pool_ln(xp, wpool, sc, g, b, alpha, seq)

            xs3 = xs.reshape(nb, t_new, d)
            ext = jnp.concatenate([state_pool[j], xs3.astype(state_pool.dtype)], axis=1)
            pools.append(ext[:, -n_state:])
            seg = POOL_HALO + t_new
            ext = jnp.concatenate([jnp.zeros((nb, seg - n_state - t_new, d), F32), ext], axis=1)
            ys = _pool_ln(ext.reshape(nb * seg, d), wpool, sc, g, b, alpha, seg)
            xs = ys.reshape(nb, seg, d)[:, seg - t_new:].reshape(nb * t_new, d)

        xp = ffn(xp, 1)
        xs = ffn(xs, 1)

        wg = ple_w_gate[i].astype(BF16)
        wp = ple_w_proj[i].astype(BF16)
        bg = row(ple_b_gate[i])
        g, b = row(ln_g[i, 3]), row(ln_b[i, 3])
        xp = _ple_ln(xp, p_prompt[i].reshape(batch * seq, -1), wg, bg, wp, g, b, alpha)
        xs = _ple_ln(xs, p_sample[i].reshape(nb * t_new, -1), wg, bg, wp, g, b, alpha)

    return (xp.reshape(batch, seq, d), xs.reshape(nb, t_new, d),
            jnp.stack(kp), jnp.stack(vp), jnp.stack(fp), jnp.stack(poolp),
            jnp.stack(ksm), jnp.stack(vsm), jnp.stack(fsm), jnp.stack(pools))
```

```python
import functools

import numpy as np
import jax
import jax.numpy as jnp
from jax import lax
from jax.experimental import pallas as pl
from jax.experimental.pallas import tpu as pltpu

F32 = jnp.float32
BF16 = jnp.bfloat16

LANES = 128
HEAD_DIM = 64
POOL_WINDOWS = (2, 4, 8, 16)
POOL_HALO = 16
LN_EPS = 1e-5
LOG2E = 1.4426950408889634
NEG_BIG = -1e30
VMEM_LIMIT = 56 * 1024 * 1024

ROW_TILE = 512
FFN_CHUNK = 256
ATTN_TILE = 512


def _params(n_axes=1):
    return pltpu.CompilerParams(
        dimension_semantics=("arbitrary",) * n_axes, vmem_limit_bytes=VMEM_LIMIT)


def _resident(shape):
    nd = len(shape)
    return pl.BlockSpec(shape, lambda *_: (0,) * nd, pipeline_mode=pl.Buffered(1))


def _rows(tm, width):
    return pl.BlockSpec((tm, width), lambda i: (i, 0))


def _sigmoid(z):
    return 1.0 / (1.0 + jnp.exp(-z))


def _post_norm(x, sub, g, b, alpha):
    y = alpha * x + sub
    mu = jnp.mean(y, axis=-1, keepdims=True)
    yc = y - mu
    var = jnp.mean(yc * yc, axis=-1, keepdims=True)
    return yc * lax.rsqrt(var + LN_EPS) * g + b


def _split3(x):
    hi = x.astype(BF16)
    r1 = x - hi.astype(F32)
    mid = r1.astype(BF16)
    lo = (r1 - mid.astype(F32)).astype(BF16)
    return hi, mid, lo


def _dot(a, b):
    return jnp.dot(a, b, preferred_element_type=F32)


def _dot_nt(a, b):
    return lax.dot_general(a, b, (((1,), (1,)), ((), ())), preferred_element_type=F32)


def _ffn_ln_kernel(x_ref, win_ref, wout_ref, g_ref, b_ref, o_ref, xb_sc, acc_sc, *, alpha):
    x = x_ref[...]
    xb = x.astype(BF16)
    acc = None
    for j in range(win_ref.shape[1]):
        a = _dot(xb, win_ref[0, j])
        u = _dot(xb, win_ref[1, j])
        act = (a * _sigmoid(a) * u).astype(BF16)
        part = _dot(act, wout_ref[j])
        acc = part if acc is None else acc + part
    o_ref[...] = _post_norm(x, 0.5 * acc, g_ref[...], b_ref[...], alpha)


def _ffn_ln(x, w_in, w_out, g, b, alpha):
    rows, d = x.shape
    tm = min(ROW_TILE, rows)
    return pl.pallas_call(
        functools.partial(_ffn_ln_kernel, alpha=alpha),
        grid=(rows // tm,),
        in_specs=[_rows(tm, d), _resident(w_in.shape), _resident(w_out.shape),
                  _resident(g.shape), _resident(b.shape)],
        out_specs=_rows(tm, d),
        out_shape=jax.ShapeDtypeStruct((rows, d), F32),
        scratch_shapes=[pltpu.VMEM((tm, d), BF16), pltpu.VMEM((tm, d), F32)],
        compiler_params=_params(),
        name="ffn_ln",
    )(x, w_in, w_out, g, b)


def _ple_ln_kernel(x_ref, p_ref, wg_ref, bg_ref, wp_ref, g_ref, b_ref, o_ref, *, alpha):
    x = x_ref[...]
    gate = _sigmoid(_dot(x.astype(BF16), wg_ref[...]) + bg_ref[...])
    proj = _dot(p_ref[...].astype(BF16), wp_ref[...])
    o_ref[...] = _post_norm(x, proj * gate, g_ref[...], b_ref[...], alpha)


def _ple_ln(x, p_all, layer, wg, bg, wp, g, b, alpha):
    rows, d = x.shape
    tm = min(ROW_TILE, rows)
    tiles = rows // tm
    return pl.pallas_call(
        functools.partial(_ple_ln_kernel, alpha=alpha),
        grid=(tiles,),
        in_specs=[_rows(tm, d), pl.BlockSpec((tm, p_all.shape[1]), lambda i: (layer * tiles + i, 0)),
                  _resident(wg.shape), _resident(bg.shape),
                  _resident(wp.shape), _resident(g.shape), _resident(b.shape)],
        out_specs=_rows(tm, d),
        out_shape=jax.ShapeDtypeStruct((rows, d), F32),
        compiler_params=_params(),
        name="ple_ln",
    )(x, p_all, wg, bg, wp, g, b)


def _proj_ln_kernel(x_ref, o_ref_in, wo_ref, g_ref, b_ref, out_ref, *, alpha):
    m = _dot(o_ref_in[...], wo_ref[...])
    out_ref[...] = _post_norm(x_ref[...], m, g_ref[...], b_ref[...], alpha)


def _proj_ln(x, o, wo, g, b, alpha):
    rows, d = x.shape
    tm = min(ROW_TILE, rows)
    return pl.pallas_call(
        functools.partial(_proj_ln_kernel, alpha=alpha),
        grid=(rows // tm,),
        in_specs=[_rows(tm, d), _rows(tm, d), _resident(wo.shape), _resident(g.shape),
                  _resident(b.shape)],
        out_specs=_rows(tm, d),
        out_shape=jax.ShapeDtypeStruct((rows, d), F32),
        compiler_params=_params(),
        name="attn_out_ln",
    )(x, o, wo, g, b)


def _pool_ln_kernel(x_ref, halo_ref, w_ref, scale_ref, g_ref, b_ref, o_ref, ext_sc,
                    *, alpha, seq_len, tm):
    i = pl.program_id(0)
    start = (i * tm) % seq_len
    x = x_ref[...]
    d = x.shape[1]
    grp = d // len(POOL_WINDOWS)
    ext_sc[0:POOL_HALO, :] = jnp.where(start == 0, 0.0, halo_ref[...])
    ext_sc[POOL_HALO:, :] = x
    pos = (start + lax.broadcasted_iota(jnp.int32, (tm, 1), 0)) % seq_len
    outs = []
    for gi, w in enumerate(POOL_WINDOWS):
        sl = slice(gi * grp, (gi + 1) * grp)
        tot = x[:, sl]
        for back in range(1, w):
            tot = tot + ext_sc[POOL_HALO - back:POOL_HALO - back + tm, sl]
        cnt = jnp.minimum(pos + 1, w).astype(F32)
        pooled = tot / cnt - x[:, sl]
        outs.append(_dot(pooled.astype(BF16), w_ref[gi]))
    y = jnp.concatenate(outs, axis=1) * scale_ref[...]
    o_ref[...] = _post_norm(x, y, g_ref[...], b_ref[...], alpha)


def _pool_ln(x, w_pool, scale, g, b, alpha, seq_len):
    rows, d = x.shape
    tm = min(ROW_TILE, rows)
    assert seq_len % tm == 0 or tm % seq_len == 0
    halo_blocks = tm // POOL_HALO
    return pl.pallas_call(
        functools.partial(_pool_ln_kernel, alpha=alpha, seq_len=seq_len, tm=tm),
        grid=(rows // tm,),
        in_specs=[_rows(tm, d),
                  pl.BlockSpec((POOL_HALO, d), lambda i: (jnp.maximum(i * halo_blocks - 1, 0), 0)),
                  _resident(w_pool.shape), _resident(scale.shape), _resident(g.shape),
                  _resident(b.shape)],
        out_specs=_rows(tm, d),
        out_shape=jax.ShapeDtypeStruct((rows, d), F32),
        scratch_shapes=[pltpu.VMEM((POOL_HALO + tm, d), F32)],
        compiler_params=_params(),
        name="pool_ln",
    )(x, x, w_pool, scale, g, b)


def _log_sigmoid(z):
    return jnp.minimum(z, 0.0) - jnp.log1p(jnp.exp(-jnp.abs(z)))


def _lane_group_select(a, b, c):
    lane = lax.broadcasted_iota(jnp.int32, a.shape, 1)
    a, b, c = a.astype(F32), b.astype(F32), c.astype(F32)
    sel = jnp.where(lane < 16, a, jnp.where(lane < 32, b, jnp.where(
        lane < 48, c, jnp.where(lane < 64, 1.0, 0.0))))
    return sel.astype(BF16)


def _expand_heads(t):
    n = t.shape[1] // LANES
    return jnp.concatenate(
        [t[:, (gidx // 2) * LANES:(gidx // 2 + 1) * LANES] for gidx in range(2 * n)], axis=1)


def _fox_proj_prompt_kernel(x_ref, wt_ref, wk_ref, wf_ref, bf_ref, tri_ref, pk_ref, psel_ref,
                            qxt_ref, kx_ref, vxt_ref, kt_ref, vt_ref, logft_ref, carry_sc,
                            *, q_scale, n_heads):
    @pl.when(pl.program_id(1) == 0)
    def _():
        carry_sc[...] = jnp.zeros_like(carry_sc)

    xb = x_ref[...].astype(BF16)
    tm, d = xb.shape
    ht = _dot_nt(wt_ref[...], xb)
    kt_ref[0] = ht[d:2 * d]
    vt_ref[0] = ht[2 * d:3 * d]
    k_tok = _dot(xb, wk_ref[...])

    logf = _log_sigmoid(_dot(xb, wf_ref[...]) + bf_ref[...])
    logft_ref[0] = logf.T[0:n_heads]

    hi, mid, lo = _split3(logf)
    cs = _dot(tri_ref[...], jnp.concatenate([hi, mid, lo], axis=1))
    cum = (cs[:, :LANES] + cs[:, LANES:2 * LANES] + cs[:, 2 * LANES:]) + carry_sc[...]
    carry_sc[...] = cum[tm - 1:tm, :]

    f2 = cum * LOG2E
    fh, fm, fl = _split3(f2)
    ak = _dot(_lane_group_select(fh, fm, fl), pk_ref[...])
    lane = lax.broadcasted_iota(jnp.int32, ak.shape, 1)
    own = (((lane >> 6) ^ (lane >> 7)) & 1) == 0
    kx_ref[...] = jnp.where(own, _expand_heads(k_tok), ak).astype(BF16)

    th, tmid, tl = _split3(f2.T[0:n_heads])
    pieces = jnp.concatenate([th, tmid, tl, jnp.ones_like(th)], axis=0)
    aq = _dot(psel_ref[...], pieces).astype(BF16)
    qt = (ht[0:d] * q_scale).astype(BF16)
    vt = ht[2 * d:3 * d].astype(BF16)
    sub = lax.broadcasted_iota(jnp.int32, (16, tm), 0)
    ones_row = jnp.where(sub == 0, 1.0, 0.0).astype(BF16)
    zeros = jnp.zeros((HEAD_DIM - 16, tm), BF16)
    for h in range(n_heads):
        q_h = qt[h * HEAD_DIM:(h + 1) * HEAD_DIM]
        a_h = aq[16 * h:16 * (h + 1)]
        parts = [q_h, a_h, zeros] if h % 2 == 0 else [a_h, zeros, q_h]
        qxt_ref[0, 0, h * LANES:(h + 1) * LANES, :] = jnp.concatenate(parts, axis=0)
        vxt_ref[0, 0, h * LANES:(h + 1) * LANES, :] = jnp.concatenate(
            [vt[h * HEAD_DIM:(h + 1) * HEAD_DIM], ones_row, zeros], axis=0)


def _aug_placement(n_heads):
    pk = np.zeros((LANES, 2 * n_heads * HEAD_DIM), np.float32)
    psel = np.zeros((16 * n_heads, 4 * n_heads), np.float32)
    for h in range(n_heads):
        base = h * LANES + (HEAD_DIM if h % 2 == 0 else 0)
        for piece in range(3):
            pk[3 * n_heads + h, base + piece] = 1.0
            pk[n_heads * piece + h, base + 3 + piece] = -1.0
            psel[16 * h + piece, n_heads * piece + h] = 1.0
            psel[16 * h + 3 + piece, 3 * n_heads + h] = 1.0
    return jnp.asarray(pk, BF16), jnp.asarray(psel, BF16)


def _fox_proj_prompt(x, wt, wk, wf, bf, batch, seq_len, q_scale, n_heads):
    rows, d = x.shape
    tm = min(ATTN_TILE, seq_len)
    nt = seq_len // tm
    tri = jnp.asarray(np.tril(np.ones((tm, tm), np.float32)), BF16)
    pk, psel = _aug_placement(n_heads)
    fm_tiles = pl.BlockSpec((1, 1, 2 * d, tm), lambda b, t: (b, t, 0, 0))
    fm_out = lambda r: pl.BlockSpec((1, r, tm), lambda b, t: (b, 0, t))
    return pl.pallas_call(
        functools.partial(_fox_proj_prompt_kernel, q_scale=q_scale, n_heads=n_heads),
        grid=(batch, nt),
        in_specs=[pl.BlockSpec((tm, d), lambda b, t: (b * nt + t, 0)), _resident(wt.shape),
                  _resident(wk.shape), _resident(wf.shape), _resident(bf.shape), _resident(tri.shape),
                  _resident(pk.shape), _resident(psel.shape)],
        out_specs=[fm_tiles, pl.BlockSpec((tm, 2 * d), lambda b, t: (b * nt + t, 0)), fm_tiles,
                   fm_out(d), fm_out(d), fm_out(n_heads)],
        out_shape=[jax.ShapeDtypeStruct((batch, nt, 2 * d, tm), BF16),
                   jax.ShapeDtypeStruct((rows, 2 * d), BF16),
                   jax.ShapeDtypeStruct((batch, nt, 2 * d, tm), BF16),
                   jax.ShapeDtypeStruct((batch, d, seq_len), F32),
                   jax.ShapeDtypeStruct((batch, d, seq_len), F32),
                   jax.ShapeDtypeStruct((batch, n_heads, seq_len), F32)],
        scratch_shapes=[pltpu.VMEM((1, LANES), F32)],
        compiler_params=_params(2),
        name="fox_proj_prompt",
    )(x, wt, wk, wf, bf, tri, pk, psel)


def _fox_proj_sample_kernel(x_ref, wq_ref, wk_ref, wv_ref, wf_ref, bf_ref,
                            q_ref, k_ref, v_ref, logf_ref, *, q_scale, n_heads):
    xb = x_ref[...].astype(BF16)
    q_ref[...] = (_dot(xb, wq_ref[...]) * q_scale).astype(BF16)
    k_ref[...] = _dot(xb, wk_ref[...])
    v_ref[...] = _dot(xb, wv_ref[...])
    logf = _log_sigmoid(_dot(xb, wf_ref[...]) + bf_ref[...])
    logf_ref[...] = logf[:, :n_heads]


def _fox_proj_sample(x, wq, wk, wv, wf, bf, q_scale, n_heads):
    rows, d = x.shape
    tm = min(ROW_TILE, rows)
    return pl.pallas_call(
        functools.partial(_fox_proj_sample_kernel, q_scale=q_scale, n_heads=n_heads),
        grid=(rows // tm,),
        in_specs=[_rows(tm, d), _resident(wq.shape), _resident(wk.shape), _resident(wv.shape),
                  _resident(wf.shape), _resident(bf.shape)],
        out_specs=[_rows(tm, d), _rows(tm, d), _rows(tm, d), _rows(tm, n_heads)],
        out_shape=[jax.ShapeDtypeStruct((rows, d), BF16), jax.ShapeDtypeStruct((rows, d), F32),
                   jax.ShapeDtypeStruct((rows, d), F32), jax.ShapeDtypeStruct((rows, n_heads), F32)],
        compiler_params=_params(),
        name="fox_proj_sample",
    )(x, wq, wk, wv, wf, bf)


def _fox_attn_prompt_kernel(kx_ref, qxt_ref, vxt_ref, o_ref, *, tile):
    n_blocks = qxt_ref.shape[1]
    key_in_block = lax.broadcasted_iota(jnp.int32, (tile, tile), 0)
    query_in_block = lax.broadcasted_iota(jnp.int32, (tile, tile), 1)
    causal = key_in_block <= query_in_block
    groups = [slice(h * LANES, (h + 1) * LANES) for h in range(2)]

    def scores(qi, kj, masked):
        k0 = pl.multiple_of(kj * tile, tile)
        out = []
        for h in range(2):
            s_t = _dot(kx_ref[pl.ds(k0, tile), groups[h]], qxt_ref[0, qi, groups[h], :])
            if masked:
                s_t = jnp.where(causal, s_t, -jnp.inf)
            out.append((s_t, jnp.max(s_t, axis=0, keepdims=True)))
        return tuple(out)

    def accumulate(kj, sc, state):
        new_state = []
        for h in range(2):
            m_prev, acc = state[h]
            s_t, m_blk = sc[h]
            m_new = jnp.maximum(m_prev, m_blk)
            alpha = jnp.exp2(m_prev - m_new)
            p_t = jnp.exp2(s_t - m_new).astype(BF16)
            acc = alpha * acc + _dot(vxt_ref[0, kj, groups[h], :], p_t)
            new_state.append((m_new, acc))
        return tuple(new_state)

    def finish(qi, state):
        o_t = jnp.concatenate(
            [acc[0:HEAD_DIM] / acc[HEAD_DIM:HEAD_DIM + 1] for _, acc in state], axis=0)
        o_ref[pl.ds(pl.multiple_of(qi * tile, tile), tile), :] = o_t.T.astype(o_ref.dtype)

    init = tuple((jnp.full((1, tile), NEG_BIG, F32), jnp.zeros((LANES, tile), F32)) for _ in range(2))

    def q_block(qi, carry):
        state = lax.fori_loop(
            0, qi, lambda kj, st: accumulate(kj, scores(qi, kj, False), st), init)
        finish(qi, accumulate(qi, scores(qi, qi, True), state))
        return carry

    lax.fori_loop(0, n_blocks, q_block, 0)


def _fox_attn_prompt(kx, qxt, vxt, batch, seq_len):
    rows, d2 = kx.shape
    d = d2 // 2
    pairs = d // LANES
    tile = qxt.shape[3]
    n_blocks = seq_len // tile
    feature_major = pl.BlockSpec((1, n_blocks, 2 * LANES, tile), lambda b, p: (b, 0, p, 0))
    return pl.pallas_call(
        functools.partial(_fox_attn_prompt_kernel, tile=tile),
        grid=(batch, pairs),
        in_specs=[pl.BlockSpec((seq_len, 2 * LANES), lambda b, p: (b, p)), feature_major, feature_major],
        out_specs=pl.BlockSpec((seq_len, LANES), lambda b, p: (b, p)),
        out_shape=jax.ShapeDtypeStruct((rows, d), BF16),
        compiler_params=_params(2),
        name="fox_attn_prompt",
    )(kx, qxt, vxt)


def _fox_attn_sample_kernel(q_ref, kn_ref, vn_ref, ckt_ref, cvt_ref, lft_ref, o_ref, *, n_heads):
    t_new, d = q_ref.shape[1], q_ref.shape[2]
    past = ckt_ref.shape[3]
    keys = lft_ref.shape[2]
    qrows = n_heads * t_new

    cum = lft_ref[0]
    key_idx = lax.broadcasted_iota(jnp.int32, cum.shape, 1)
    step = 1
    while step < keys:
        cum = cum + jnp.where(key_idx >= step, pltpu.roll(cum, step, axis=1), 0.0)
        step *= 2

    r_head = lax.broadcasted_iota(jnp.int32, (qrows, n_heads), 0) // t_new
    rep_head = jnp.where(r_head == lax.broadcasted_iota(jnp.int32, (qrows, n_heads), 1),
                         1.0, 0.0).astype(BF16)
    r_query = lax.broadcasted_iota(jnp.int32, (qrows, t_new), 0) % t_new
    rep_query = jnp.where(r_query == lax.broadcasted_iota(jnp.int32, (qrows, t_new), 1),
                          1.0, 0.0).astype(BF16)

    c_hi, c_mid, c_lo = _split3(cum * LOG2E)
    cum_rows = _dot(rep_head, c_hi) + _dot(rep_head, c_mid) + _dot(rep_head, c_lo)
    kidx = lax.broadcasted_iota(jnp.int32, (qrows, keys), 1)
    qpos = past + lax.broadcasted_iota(jnp.int32, (qrows, keys), 0) % t_new
    cum_q = jnp.sum(jnp.where(kidx == qpos, cum_rows, 0.0), axis=1, keepdims=True)

    lane_head = lax.broadcasted_iota(jnp.int32, (qrows, d), 1) // HEAD_DIM
    row_head = lax.broadcasted_iota(jnp.int32, (qrows, d), 0) // t_new
    q_rep = _dot(rep_query, q_ref[0])
    q_exp = jnp.where(lane_head == row_head, q_rep, 0.0).astype(BF16)

    bias = cum_q - cum_rows
    s_old = _dot(q_exp, ckt_ref[0, 0].astype(BF16)) + bias[:, 0:past]
    s_new = _dot_nt(q_exp, kn_ref[0].astype(BF16)) + bias[:, past:past + t_new]
    new_key = lax.broadcasted_iota(jnp.int32, (qrows, t_new), 1)
    new_query = lax.broadcasted_iota(jnp.int32, (qrows, t_new), 0) % t_new
    s_new = jnp.where(new_key <= new_query, s_new, -jnp.inf)
    m = jnp.maximum(jnp.max(s_old, axis=1, keepdims=True), jnp.max(s_new, axis=1, keepdims=True))
    p_old = jnp.exp2(s_old - m)
    p_new = jnp.exp2(s_new - m)
    l = jnp.sum(p_old, axis=1, keepdims=True) + jnp.sum(p_new, axis=1, keepdims=True)
    o_all = (_dot_nt(p_old.astype(BF16), cvt_ref[0, 0].astype(BF16))
             + _dot(p_new.astype(BF16), vn_ref[0].astype(BF16))) / l
    o_all = jnp.where(lane_head == row_head, o_all, 0.0)
    out = o_all[0:t_new, :]
    for h in range(1, n_heads):
        out = out + o_all[h * t_new:(h + 1) * t_new, :]
    o_ref[0] = out.astype(o_ref.dtype)


def _fox_attn_sample(q, k_new, v_new, cache_kt, cache_vt, layer, lft, n_heads):
    nb, t_new, d = q.shape
    past = cache_kt.shape[3]
    keys = lft.shape[2]
    per_stream = lambda shape: pl.BlockSpec((1,) + shape, lambda b: (b, 0, 0))
    cache = pl.BlockSpec((1, 1, d, past), lambda b: (layer, b, 0, 0))
    return pl.pallas_call(
        functools.partial(_fox_attn_sample_kernel, n_heads=n_heads),
        grid=(nb,),
        in_specs=[per_stream((t_new, d)), per_stream((t_new, d)), per_stream((t_new, d)),
                  cache, cache, per_stream((n_heads, keys))],
        out_specs=per_stream((t_new, d)),
        out_shape=jax.ShapeDtypeStruct((nb, t_new, d), BF16),
        compiler_params=_params(),
        name="fox_attn_sample",
    )(q, k_new, v_new, cache_kt, cache_vt, lft)


def kernel(x_prompt, x_sample, cache_fox_k, cache_fox_v, cache_fox_logf, state_pool, p_prompt, p_sample,
           ln_g, ln_b, ffn_w_in, ffn_w_out, fox_w_in, fox_b_f, fox_w_o, pool_w, pool_scale,
           ple_w_proj, ple_w_gate, ple_b_gate):
    batch, seq, d = x_prompt.shape
    nb, t_new, _ = x_sample.shape
    depth = ln_g.shape[0]
    n_heads = d // HEAD_DIM
    d_ff = ffn_w_out.shape[2]
    past = cache_fox_k.shape[2]
    n_state = state_pool.shape[2]
    alpha = (2.0 * depth) ** 0.25
    q_scale = HEAD_DIM ** -0.5 * LOG2E
    n_chunks = d_ff // FFN_CHUNK

    xp = x_prompt.reshape(batch * seq, d)
    xs = x_sample.reshape(nb * t_new, d)
    row = lambda a: a.reshape(1, -1)
    p_prompt_rows = p_prompt.reshape(depth * batch * seq, -1)
    p_sample_rows = p_sample.reshape(depth * nb * t_new, -1)
    cache_kt = jnp.transpose(cache_fox_k, (0, 1, 3, 4, 2)).reshape(-1, nb, d, past)
    cache_vt = jnp.transpose(cache_fox_v, (0, 1, 3, 4, 2)).reshape(-1, nb, d, past)

    kp, vp, fp, poolp, ksm, vsm, fsm, pools = [], [], [], [], [], [], [], []
    for i in range(depth):
        j = i // 2

        def ffn(x, s):
            w_in = ffn_w_in[i, s].astype(BF16).reshape(d, 2, n_chunks, FFN_CHUNK).transpose(1, 2, 0, 3)
            w_out = ffn_w_out[i, s].astype(BF16).reshape(n_chunks, FFN_CHUNK, d)
            return _ffn_ln(x, w_in, w_out, row(ln_g[i, 2 * s]), row(ln_b[i, 2 * s]), alpha)

        xp = ffn(xp, 0)
        xs = ffn(xs, 0)

        if i % 2 == 0:
            w = fox_w_in[j]
            wq, wk, wv = (w[:, c * d:(c + 1) * d].astype(BF16) for c in range(3))
            reps = LANES // n_heads
            wf = jnp.tile(w[:, 3 * d:], (1, reps)).astype(BF16)
            bf = jnp.tile(fox_b_f[j], reps).reshape(1, LANES)
            wo = fox_w_o[j].astype(BF16)
            g, b = row(ln_g[i, 1]), row(ln_b[i, 1])

            wt = jnp.transpose(w)[:3 * d].astype(BF16)
            qxt, kx, vxt, kt, vt, ft = _fox_proj_prompt(xp, wt, wk, wf, bf, batch, seq, q_scale, n_heads)
            o = _fox_attn_prompt(kx, qxt, vxt, batch, seq)
            xp = _proj_ln(xp, o, wo, g, b, alpha)
            kp.append(kt)
            vp.append(vt)
            fp.append(ft)

            q2, k2, v2, f2 = _fox_proj_sample(xs, wq, wk, wv, wf, bf, q_scale, n_heads)
            keys = -(-(past + t_new) // LANES) * LANES
            lft = jnp.concatenate(
                [cache_fox_logf[j].astype(F32), f2.reshape(nb, t_new, n_heads),
                 jnp.zeros((nb, keys - past - t_new, n_heads), F32)], axis=1).transpose(0, 2, 1)
            o2 = _fox_attn_sample(q2.reshape(nb, t_new, d), k2.reshape(nb, t_new, d),
                                  v2.reshape(nb, t_new, d), cache_kt, cache_vt, j, lft, n_heads)
            xs = _proj_ln(xs, o2.reshape(nb * t_new, d), wo, g, b, alpha)
            ksm.append(k2.reshape(nb, t_new, n_heads, HEAD_DIM))
            vsm.append(v2.reshape(nb, t_new, n_heads, HEAD_DIM))
            fsm.append(f2.reshape(nb, t_new, n_heads))
        else:
            wpool = pool_w[j].astype(BF16)
            sc = row(pool_scale[j])
            g, b = row(ln_g[i, 1]), row(ln_b[i, 1])
            poolp.append(xp.reshape(batch, seq, d)[:, seq - n_state:])
            xp = _pool_ln(xp, wpool, sc, g, b, alpha, seq)

            xs3 = xs.reshape(nb, t_new, d)
            ext = jnp.concatenate([state_pool[j], xs3.astype(state_pool.dtype)], axis=1)
            pools.append(ext[:, -n_state:])
            seg = POOL_HALO + t_new
            ext = jnp.concatenate([jnp.zeros((nb, seg - n_state - t_new, d), F32), ext], axis=1)
            ys = _pool_ln(ext.reshape(nb * seg, d), wpool, sc, g, b, alpha, seg)
            xs = ys.reshape(nb, seg, d)[:, seg - t_new:].reshape(nb * t_new, d)

        xp = ffn(xp, 1)
        xs = ffn(xs, 1)

        wg = ple_w_gate[i].astype(BF16)
        wp = ple_w_proj[i].astype(BF16)
        bg = row(ple_b_gate[i])
        g, b = row(ln_g[i, 3]), row(ln_b[i, 3])
        xp = _ple_ln(xp, p_prompt_rows, i, wg, bg, wp, g, b, alpha)
        xs = _ple_ln(xs, p_sample_rows, i, wg, bg, wp, g, b, alpha)

    def token_major(stacked):
        return jnp.transpose(stacked.reshape(-1, batch, n_heads, HEAD_DIM, seq), (0, 1, 4, 2, 3))

    return (xp.reshape(batch, seq, d), xs.reshape(nb, t_new, d),
            token_major(jnp.stack(kp)), token_major(jnp.stack(vp)),
            jnp.transpose(jnp.stack(fp), (0, 1, 3, 2)), jnp.stack(poolp),
            jnp.stack(ksm), jnp.stack(vsm), jnp.stack(fsm), jnp.stack(pools))
```

```python
import functools

import numpy as np
import jax
import jax.numpy as jnp
from jax import lax
from jax.experimental import pallas as pl
from jax.experimental.pallas import tpu as pltpu

F32 = jnp.float32
BF16 = jnp.bfloat16

LANES = 128
HEAD_DIM = 64
POOL_WINDOWS = (2, 4, 8, 16)
POOL_HALO = 16
LN_EPS = 1e-5
LOG2E = 1.4426950408889634
NEG_BIG = -1e30
VMEM_LIMIT = 56 * 1024 * 1024

ROW_TILE = 512
FFN_CHUNK = 256
ATTN_TILE = 512


def _params(n_axes=1):
    return pltpu.CompilerParams(
        dimension_semantics=("arbitrary",) * n_axes, vmem_limit_bytes=VMEM_LIMIT)


def _resident(shape):
    nd = len(shape)
    return pl.BlockSpec(shape, lambda *_: (0,) * nd, pipeline_mode=pl.Buffered(1))


def _rows(tm, width):
    return pl.BlockSpec((tm, width), lambda i: (i, 0))


def _sigmoid(z):
    return 1.0 / (1.0 + jnp.exp(-z))


def _post_norm(x, sub, g, b, alpha):
    y = alpha * x + sub
    mu = jnp.mean(y, axis=-1, keepdims=True)
    yc = y - mu
    var = jnp.mean(yc * yc, axis=-1, keepdims=True)
    return yc * lax.rsqrt(var + LN_EPS) * g + b


def _split3(x):
    hi = x.astype(BF16)
    r1 = x - hi.astype(F32)
    mid = r1.astype(BF16)
    lo = (r1 - mid.astype(F32)).astype(BF16)
    return hi, mid, lo


def _dot(a, b):
    return jnp.dot(a, b, preferred_element_type=F32)


def _dot_nt(a, b):
    return lax.dot_general(a, b, (((1,), (1,)), ((), ())), preferred_element_type=F32)


def _swiglu(x, win_ref, wout_ref):
    xb = x.astype(BF16)
    acc = None
    for j in range(win_ref.shape[1]):
        a = _dot(xb, win_ref[0, j])
        u = _dot(xb, win_ref[1, j])
        act = (a * _sigmoid(a) * u).astype(BF16)
        part = _dot(act, wout_ref[j])
        acc = part if acc is None else acc + part
    return acc


def _ffn_ln_kernel(x_ref, win_ref, wout_ref, g_ref, b_ref, o_ref, *, alpha):
    x = x_ref[...]
    o_ref[...] = _post_norm(x, 0.5 * _swiglu(x, win_ref, wout_ref), g_ref[...], b_ref[...], alpha)


def _ffn_ln(x, w_in, w_out, g, b, alpha):
    rows, d = x.shape
    tm = min(ROW_TILE, rows)
    return pl.pallas_call(
        functools.partial(_ffn_ln_kernel, alpha=alpha),
        grid=(rows // tm,),
        in_specs=[_rows(tm, d), _resident(w_in.shape), _resident(w_out.shape),
                  _resident(g.shape), _resident(b.shape)],
        out_specs=_rows(tm, d),
        out_shape=jax.ShapeDtypeStruct((rows, d), F32),
        compiler_params=_params(),
        name="ffn_ln",
    )(x, w_in, w_out, g, b)


def _pool_mix(x, halo_ref, w_ref, scale_ref, ext_sc, seq_len):
    tm, d = x.shape
    start = (pl.program_id(0) * tm) % seq_len
    grp = d // len(POOL_WINDOWS)
    ext_sc[0:POOL_HALO, :] = jnp.where(start == 0, 0.0, halo_ref[...])
    ext_sc[POOL_HALO:, :] = x
    pos = (start + lax.broadcasted_iota(jnp.int32, (tm, 1), 0)) % seq_len
    outs = []
    for gi, w in enumerate(POOL_WINDOWS):
        sl = slice(gi * grp, (gi + 1) * grp)
        tot = x[:, sl]
        for back in range(1, w):
            tot = tot + ext_sc[POOL_HALO - back:POOL_HALO - back + tm, sl]
        cnt = jnp.minimum(pos + 1, w).astype(F32)
        pooled = tot / cnt - x[:, sl]
        outs.append(_dot(pooled.astype(BF16), w_ref[gi]))
    return jnp.concatenate(outs, axis=1) * scale_ref[...]


def _pool_halo_spec(tm, d):
    halo_blocks = tm // POOL_HALO
    return pl.BlockSpec((POOL_HALO, d), lambda i: (jnp.maximum(i * halo_blocks - 1, 0), 0))


def _pool_ln_kernel(x_ref, halo_ref, w_ref, scale_ref, g_ref, b_ref, o_ref, ext_sc, *, alpha, seq_len):
    x = x_ref[...]
    y = _pool_mix(x, halo_ref, w_ref, scale_ref, ext_sc, seq_len)
    o_ref[...] = _post_norm(x, y, g_ref[...], b_ref[...], alpha)


def _pool_ln(x, w_pool, scale, g, b, alpha, seq_len):
    rows, d = x.shape
    tm = min(ROW_TILE, rows)
    assert seq_len % tm == 0 or tm % seq_len == 0
    return pl.pallas_call(
        functools.partial(_pool_ln_kernel, alpha=alpha, seq_len=seq_len),
        grid=(rows // tm,),
        in_specs=[_rows(tm, d), _pool_halo_spec(tm, d), _resident(w_pool.shape), _resident(scale.shape),
                  _resident(g.shape), _resident(b.shape)],
        out_specs=_rows(tm, d),
        out_shape=jax.ShapeDtypeStruct((rows, d), F32),
        scratch_shapes=[pltpu.VMEM((POOL_HALO + tm, d), F32)],
        compiler_params=_params(),
        name="pool_ln",
    )(x, x, w_pool, scale, g, b)


def _tail_kernel(*refs, mixer, alpha, seq_len):
    x_ref, refs = refs[0], refs[1:]
    n_mix = {"fox": 2, "pool": 3, "none": 0}[mixer]
    mix_refs, refs = refs[:n_mix], refs[n_mix:]
    win_ref, wout_ref, p_ref, wg_ref, bg_ref, wp_ref, g_ref, b_ref, o_ref = refs[:9]
    x = x_ref[...]
    if mixer == "fox":
        attn_ref, wo_ref = mix_refs
        x = _post_norm(x, _dot(attn_ref[...], wo_ref[...]), g_ref[0:1], b_ref[0:1], alpha)
    elif mixer == "pool":
        halo_ref, wpool_ref, scale_ref = mix_refs
        y = _pool_mix(x, halo_ref, wpool_ref, scale_ref, refs[9], seq_len)
        x = _post_norm(x, y, g_ref[0:1], b_ref[0:1], alpha)
    x = _post_norm(x, 0.5 * _swiglu(x, win_ref, wout_ref), g_ref[1:2], b_ref[1:2], alpha)
    gate = _sigmoid(_dot(x.astype(BF16), wg_ref[...]) + bg_ref[...])
    proj = _dot(p_ref[...].astype(BF16), wp_ref[...])
    o_ref[...] = _post_norm(x, proj * gate, g_ref[2:3], b_ref[2:3], alpha)


def _tail(x, mixer, mix_args, w_in, w_out, p_all, layer, wg, bg, wp, g3, b3, alpha, seq_len=None):
    rows, d = x.shape
    tm = min(ROW_TILE, rows)
    tiles = rows // tm
    if mixer == "fox":
        attn, wo = mix_args
        mix_specs, mix_ops = [_rows(tm, d), _resident(wo.shape)], [attn, wo]
    elif mixer == "pool":
        w_pool, scale = mix_args
        mix_specs = [_pool_halo_spec(tm, d), _resident(w_pool.shape), _resident(scale.shape)]
        mix_ops = [x, w_pool, scale]
    else:
        mix_specs, mix_ops = [], []
    return pl.pallas_call(
        functools.partial(_tail_kernel, mixer=mixer, alpha=alpha, seq_len=seq_len),
        grid=(tiles,),
        in_specs=[_rows(tm, d)] + mix_specs + [
            _resident(w_in.shape), _resident(w_out.shape),
            pl.BlockSpec((tm, p_all.shape[1]), lambda i: (layer * tiles + i, 0)),
            _resident(wg.shape), _resident(bg.shape), _resident(wp.shape),
            _resident(g3.shape), _resident(b3.shape)],
        out_specs=_rows(tm, d),
        out_shape=jax.ShapeDtypeStruct((rows, d), F32),
        scratch_shapes=[pltpu.VMEM((POOL_HALO + tm, d), F32)] if mixer == "pool" else [],
        compiler_params=_params(),
        name="tail_" + mixer,
    )(x, *mix_ops, w_in, w_out, p_all, wg, bg, wp, g3, b3)


def _log_sigmoid(z):
    return jnp.minimum(z, 0.0) - jnp.log1p(jnp.exp(-jnp.abs(z)))


def _lane_group_select(a, b, c):
    lane = lax.broadcasted_iota(jnp.int32, a.shape, 1)
    a, b, c = a.astype(F32), b.astype(F32), c.astype(F32)
    sel = jnp.where(lane < 16, a, jnp.where(lane < 32, b, jnp.where(
        lane < 48, c, jnp.where(lane < 64, 1.0, 0.0))))
    return sel.astype(BF16)


def _expand_heads(t):
    n = t.shape[1] // LANES
    return jnp.concatenate(
        [t[:, (gidx // 2) * LANES:(gidx // 2 + 1) * LANES] for gidx in range(2 * n)], axis=1)


def _fox_proj_prompt_kernel(x_ref, wt_ref, wk_ref, wf_ref, bf_ref, tri_ref, pk_ref, psel_ref, *rest,
                            q_scale, n_heads, first_layer):
    qxt_ref, kx_ref, vxt_ref, kt_ref, vt_ref, logft_ref, carry_sc = rest[-7:]

    @pl.when(pl.program_id(1) == 0)
    def _():
        carry_sc[...] = jnp.zeros_like(carry_sc)

    xb = x_ref[...].astype(BF16)
    tm, d = xb.shape
    ht = _dot_nt(wt_ref[...], xb)
    kt_ref[0, 0] = ht[d:2 * d]
    vt_ref[0, 0] = ht[2 * d:3 * d]
    if first_layer:
        for later in range(1, kt_ref.shape[0]):
            kt_ref[later, 0] = jnp.zeros((d, tm), F32)
            vt_ref[later, 0] = jnp.zeros((d, tm), F32)
    k_tok = _dot(xb, wk_ref[...])

    logf = _log_sigmoid(_dot(xb, wf_ref[...]) + bf_ref[...])
    logft_ref[0] = logf.T[0:n_heads]

    hi, mid, lo = _split3(logf)
    cs = _dot(tri_ref[...], jnp.concatenate([hi, mid, lo], axis=1))
    cum = (cs[:, :LANES] + cs[:, LANES:2 * LANES] + cs[:, 2 * LANES:]) + carry_sc[...]
    carry_sc[...] = cum[tm - 1:tm, :]

    f2 = cum * LOG2E
    fh, fm, fl = _split3(f2)
    ak = _dot(_lane_group_select(fh, fm, fl), pk_ref[...])
    lane = lax.broadcasted_iota(jnp.int32, ak.shape, 1)
    own = (((lane >> 6) ^ (lane >> 7)) & 1) == 0
    kx_ref[...] = jnp.where(own, _expand_heads(k_tok), ak).astype(BF16)

    th, tmid, tl = _split3(f2.T[0:n_heads])
    pieces = jnp.concatenate([th, tmid, tl, jnp.ones_like(th)], axis=0)
    aq = _dot(psel_ref[...], pieces).astype(BF16)
    qt = (ht[0:d] * q_scale).astype(BF16)
    vt = ht[2 * d:3 * d].astype(BF16)
    sub = lax.broadcasted_iota(jnp.int32, (16, tm), 0)
    ones_row = jnp.where(sub == 0, 1.0, 0.0).astype(BF16)
    zeros = jnp.zeros((HEAD_DIM - 16, tm), BF16)
    for h in range(n_heads):
        q_h = qt[h * HEAD_DIM:(h + 1) * HEAD_DIM]
        a_h = aq[16 * h:16 * (h + 1)]
        parts = [q_h, a_h, zeros] if h % 2 == 0 else [a_h, zeros, q_h]
        qxt_ref[0, 0, h * LANES:(h + 1) * LANES, :] = jnp.concatenate(parts, axis=0)
        vxt_ref[0, 0, h * LANES:(h + 1) * LANES, :] = jnp.concatenate(
            [vt[h * HEAD_DIM:(h + 1) * HEAD_DIM], ones_row, zeros], axis=0)


def _aug_placement(n_heads):
    pk = np.zeros((LANES, 2 * n_heads * HEAD_DIM), np.float32)
    psel = np.zeros((16 * n_heads, 4 * n_heads), np.float32)
    for h in range(n_heads):
        base = h * LANES + (HEAD_DIM if h % 2 == 0 else 0)
        for piece in range(3):
            pk[3 * n_heads + h, base + piece] = 1.0
            pk[n_heads * piece + h, base + 3 + piece] = -1.0
            psel[16 * h + piece, n_heads * piece + h] = 1.0
            psel[16 * h + 3 + piece, 3 * n_heads + h] = 1.0
    return jnp.asarray(pk, BF16), jnp.asarray(psel, BF16)


def _fox_proj_prompt(x, wt, wk, wf, bf, batch, seq_len, q_scale, n_heads, layer, n_layers, kv_all):
    rows, d = x.shape
    tm = min(ATTN_TILE, seq_len)
    nt = seq_len // tm
    tri = jnp.asarray(np.tril(np.ones((tm, tm), np.float32)), BF16)
    pk, psel = _aug_placement(n_heads)
    fm_tiles = pl.BlockSpec((1, 1, 2 * d, tm), lambda b, t: (b, t, 0, 0))
    first = kv_all is None
    assert first == (layer == 0)
    kv_spec = pl.BlockSpec((n_layers if first else 1, 1, d, tm), lambda b, t: (layer, b, 0, t))
    kv_shape = jax.ShapeDtypeStruct((n_layers, batch, d, seq_len), F32)
    n_in = 8
    return pl.pallas_call(
        functools.partial(_fox_proj_prompt_kernel, q_scale=q_scale, n_heads=n_heads, first_layer=first),
        grid=(batch, nt),
        in_specs=[pl.BlockSpec((tm, d), lambda b, t: (b * nt + t, 0)), _resident(wt.shape),
                  _resident(wk.shape), _resident(wf.shape), _resident(bf.shape), _resident(tri.shape),
                  _resident(pk.shape), _resident(psel.shape)]
                 + ([] if first else [pl.BlockSpec(memory_space=pl.ANY)] * 2),
        out_specs=[fm_tiles, pl.BlockSpec((tm, 2 * d), lambda b, t: (b * nt + t, 0)), fm_tiles,
                   kv_spec, kv_spec, pl.BlockSpec((1, n_heads, tm), lambda b, t: (b, 0, t))],
        out_shape=[jax.ShapeDtypeStruct((batch, nt, 2 * d, tm), BF16),
                   jax.ShapeDtypeStruct((rows, 2 * d), BF16),
                   jax.ShapeDtypeStruct((batch, nt, 2 * d, tm), BF16),
                   kv_shape, kv_shape,
                   jax.ShapeDtypeStruct((batch, n_heads, seq_len), F32)],
        input_output_aliases={} if first else {n_in: 3, n_in + 1: 4},
        scratch_shapes=[pltpu.VMEM((1, LANES), F32)],
        compiler_params=_params(2),
        name="fox_proj_prompt",
    )(x, wt, wk, wf, bf, tri, pk, psel, *([] if first else kv_all))


def _fox_proj_sample_kernel(x_ref, wq_ref, wk_ref, wv_ref, wf_ref, bf_ref,
                            q_ref, k_ref, v_ref, logf_ref, *, q_scale, n_heads):
    xb = x_ref[...].astype(BF16)
    q_ref[...] = (_dot(xb, wq_ref[...]) * q_scale).astype(BF16)
    k_ref[...] = _dot(xb, wk_ref[...])
    v_ref[...] = _dot(xb, wv_ref[...])
    logf = _log_sigmoid(_dot(xb, wf_ref[...]) + bf_ref[...])
    logf_ref[...] = logf[:, :n_heads]


def _fox_proj_sample(x, wq, wk, wv, wf, bf, q_scale, n_heads):
    rows, d = x.shape
    tm = min(ROW_TILE, rows)
    return pl.pallas_call(
        functools.partial(_fox_proj_sample_kernel, q_scale=q_scale, n_heads=n_heads),
        grid=(rows // tm,),
        in_specs=[_rows(tm, d), _resident(wq.shape), _resident(wk.shape), _resident(wv.shape),
                  _resident(wf.shape), _resident(bf.shape)],
        out_specs=[_rows(tm, d), _rows(tm, d), _rows(tm, d), _rows(tm, n_heads)],
        out_shape=[jax.ShapeDtypeStruct((rows, d), BF16), jax.ShapeDtypeStruct((rows, d), F32),
                   jax.ShapeDtypeStruct((rows, d), F32), jax.ShapeDtypeStruct((rows, n_heads), F32)],
        compiler_params=_params(),
        name="fox_proj_sample",
    )(x, wq, wk, wv, wf, bf)


def _fox_attn_prompt_kernel(kx_ref, qxt_ref, vxt_ref, o_ref, *, tile):
    n_blocks = qxt_ref.shape[1]
    key_in_block = lax.broadcasted_iota(jnp.int32, (tile, tile), 0)
    query_in_block = lax.broadcasted_iota(jnp.int32, (tile, tile), 1)
    causal = key_in_block <= query_in_block
    groups = [slice(h * LANES, (h + 1) * LANES) for h in range(2)]

    def scores(qi, kj, masked):
        k0 = pl.multiple_of(kj * tile, tile)
        out = []
        for h in range(2):
            s_t = _dot(kx_ref[pl.ds(k0, tile), groups[h]], qxt_ref[0, qi, groups[h], :])
            if masked:
                s_t = jnp.where(causal, s_t, -jnp.inf)
            out.append((s_t, jnp.max(s_t, axis=0, keepdims=True)))
        return tuple(out)

    def accumulate(kj, sc, state):
        new_state = []
        for h in range(2):
            m_prev, acc = state[h]
            s_t, m_blk = sc[h]
            m_new = jnp.maximum(m_prev, m_blk)
            alpha = jnp.exp2(m_prev - m_new)
            p_t = jnp.exp2(s_t - m_new).astype(BF16)
            acc = alpha * acc + _dot(vxt_ref[0, kj, groups[h], :], p_t)
            new_state.append((m_new, acc))
        return tuple(new_state)

    def finish(qi, state):
        o_t = jnp.concatenate(
            [acc[0:HEAD_DIM] / acc[HEAD_DIM:HEAD_DIM + 1] for _, acc in state], axis=0)
        o_ref[pl.ds(pl.multiple_of(qi * tile, tile), tile), :] = o_t.T.astype(o_ref.dtype)

    init = tuple((jnp.full((1, tile), NEG_BIG, F32), jnp.zeros((LANES, tile), F32)) for _ in range(2))

    def merge(a, b):
        out = []
        for (m_a, acc_a), (m_b, acc_b) in zip(a, b):
            m = jnp.maximum(m_a, m_b)
            out.append((m, jnp.exp2(m_a - m) * acc_a + jnp.exp2(m_b - m) * acc_b))
        return tuple(out)

    def q_block(qi, carry):
        def pair(jp, both):
            sc_a = scores(qi, 2 * jp, False)
            sc_b = scores(qi, 2 * jp + 1, False)
            return accumulate(2 * jp, sc_a, both[0]), accumulate(2 * jp + 1, sc_b, both[1])

        st_a, st_b = lax.fori_loop(0, qi // 2, pair, (init, init))
        st_b = lax.fori_loop(
            0, qi % 2, lambda _, st: accumulate(qi - 1, scores(qi, qi - 1, False), st), st_b)
        st_a = accumulate(qi, scores(qi, qi, True), st_a)
        finish(qi, merge(st_a, st_b))
        return carry

    lax.fori_loop(0, n_blocks, q_block, 0)


def _fox_attn_prompt(kx, qxt, vxt, batch, seq_len):
    rows, d2 = kx.shape
    d = d2 // 2
    pairs = d // LANES
    tile = qxt.shape[3]
    n_blocks = seq_len // tile
    feature_major = pl.BlockSpec((1, n_blocks, 2 * LANES, tile), lambda b, p: (b, 0, p, 0))
    return pl.pallas_call(
        functools.partial(_fox_attn_prompt_kernel, tile=tile),
        grid=(batch, pairs),
        in_specs=[pl.BlockSpec((seq_len, 2 * LANES), lambda b, p: (b, p)), feature_major, feature_major],
        out_specs=pl.BlockSpec((seq_len, LANES), lambda b, p: (b, p)),
        out_shape=jax.ShapeDtypeStruct((rows, d), BF16),
        compiler_params=_params(2),
        name="fox_attn_prompt",
    )(kx, qxt, vxt)


def _fox_attn_sample_kernel(q_ref, kn_ref, vn_ref, ckt_ref, cvt_ref, lft_ref, o_ref, *, n_heads):
    t_new, d = q_ref.shape[1], q_ref.shape[2]
    past = ckt_ref.shape[3]
    keys = lft_ref.shape[2]
    qrows = n_heads * t_new

    cum = lft_ref[0]
    key_idx = lax.broadcasted_iota(jnp.int32, cum.shape, 1)
    step = 1
    while step < keys:
        cum = cum + jnp.where(key_idx >= step, pltpu.roll(cum, step, axis=1), 0.0)
        step *= 2

    r_head = lax.broadcasted_iota(jnp.int32, (qrows, n_heads), 0) // t_new
    rep_head = jnp.where(r_head == lax.broadcasted_iota(jnp.int32, (qrows, n_heads), 1),
                         1.0, 0.0).astype(BF16)
    r_query = lax.broadcasted_iota(jnp.int32, (qrows, t_new), 0) % t_new
    rep_query = jnp.where(r_query == lax.broadcasted_iota(jnp.int32, (qrows, t_new), 1),
                          1.0, 0.0).astype(BF16)

    c_hi, c_mid, c_lo = _split3(cum * LOG2E)
    cum_rows = _dot(rep_head, c_hi) + _dot(rep_head, c_mid) + _dot(rep_head, c_lo)
    kidx = lax.broadcasted_iota(jnp.int32, (qrows, keys), 1)
    qpos = past + lax.broadcasted_iota(jnp.int32, (qrows, keys), 0) % t_new
    cum_q = jnp.sum(jnp.where(kidx == qpos, cum_rows, 0.0), axis=1, keepdims=True)

    lane_head = lax.broadcasted_iota(jnp.int32, (qrows, d), 1) // HEAD_DIM
    row_head = lax.broadcasted_iota(jnp.int32, (qrows, d), 0) // t_new
    q_rep = _dot(rep_query, q_ref[0])
    q_exp = jnp.where(lane_head == row_head, q_rep, 0.0).astype(BF16)

    bias = cum_q - cum_rows
    s_old = _dot(q_exp, ckt_ref[0, 0].astype(BF16)) + bias[:, 0:past]
    s_new = _dot_nt(q_exp, kn_ref[0].astype(BF16)) + bias[:, past:past + t_new]
    new_key = lax.broadcasted_iota(jnp.int32, (qrows, t_new), 1)
    new_query = lax.broadcasted_iota(jnp.int32, (qrows, t_new), 0) % t_new
    s_new = jnp.where(new_key <= new_query, s_new, -jnp.inf)
    m = jnp.maximum(jnp.max(s_old, axis=1, keepdims=True), jnp.max(s_new, axis=1, keepdims=True))
    p_old = jnp.exp2(s_old - m)
    p_new = jnp.exp2(s_new - m)
    l = jnp.sum(p_old, axis=1, keepdims=True) + jnp.sum(p_new, axis=1, keepdims=True)
    o_all = (_dot_nt(p_old.astype(BF16), cvt_ref[0, 0].astype(BF16))
             + _dot(p_new.astype(BF16), vn_ref[0].astype(BF16))) / l
    o_all = jnp.where(lane_head == row_head, o_all, 0.0)
    out = o_all[0:t_new, :]
    for h in range(1, n_heads):
        out = out + o_all[h * t_new:(h + 1) * t_new, :]
    o_ref[0] = out.astype(o_ref.dtype)


def _fox_attn_sample(q, k_new, v_new, cache_kt, cache_vt, layer, lft, n_heads):
    nb, t_new, d = q.shape
    past = cache_kt.shape[3]
    keys = lft.shape[2]
    per_stream = lambda shape: pl.BlockSpec((1,) + shape, lambda b: (b, 0, 0))
    cache = pl.BlockSpec((1, 1, d, past), lambda b: (layer, b, 0, 0))
    return pl.pallas_call(
        functools.partial(_fox_attn_sample_kernel, n_heads=n_heads),
        grid=(nb,),
        in_specs=[per_stream((t_new, d)), per_stream((t_new, d)), per_stream((t_new, d)),
                  cache, cache, per_stream((n_heads, keys))],
        out_specs=per_stream((t_new, d)),
        out_shape=jax.ShapeDtypeStruct((nb, t_new, d), BF16),
        compiler_params=_params(),
        name="fox_attn_sample",
    )(q, k_new, v_new, cache_kt, cache_vt, lft)


def kernel(x_prompt, x_sample, cache_fox_k, cache_fox_v, cache_fox_logf, state_pool, p_prompt, p_sample,
           ln_g, ln_b, ffn_w_in, ffn_w_out, fox_w_in, fox_b_f, fox_w_o, pool_w, pool_scale,
           ple_w_proj, ple_w_gate, ple_b_gate):
    batch, seq, d = x_prompt.shape
    nb, t_new, _ = x_sample.shape
    depth = ln_g.shape[0]
    n_heads = d // HEAD_DIM
    d_ff = ffn_w_out.shape[2]
    past = cache_fox_k.shape[2]
    n_state = state_pool.shape[2]
    alpha = (2.0 * depth) ** 0.25
    q_scale = HEAD_DIM ** -0.5 * LOG2E
    n_chunks = d_ff // FFN_CHUNK

    xp = x_prompt.reshape(batch * seq, d)
    xs = x_sample.reshape(nb * t_new, d)
    row = lambda a: a.reshape(1, -1)
    p_prompt_rows = p_prompt.reshape(depth * batch * seq, -1)
    p_sample_rows = p_sample.reshape(depth * nb * t_new, -1)
    cache_kt = jnp.transpose(cache_fox_k, (0, 1, 3, 4, 2)).reshape(-1, nb, d, past)
    cache_vt = jnp.transpose(cache_fox_v, (0, 1, 3, 4, 2)).reshape(-1, nb, d, past)

    n_fox = (depth + 1) // 2
    kv_all = None
    fp, poolp, ksm, vsm, fsm, pools = [], [], [], [], [], []
    for i in range(depth):
        j = i // 2

        def ffn_weights(s):
            w_in = ffn_w_in[i, s].astype(BF16).reshape(d, 2, n_chunks, FFN_CHUNK).transpose(1, 2, 0, 3)
            w_out = ffn_w_out[i, s].astype(BF16).reshape(n_chunks, FFN_CHUNK, d)
            return w_in, w_out

        w_in1, w_out1 = ffn_weights(0)
        xp = _ffn_ln(xp, w_in1, w_out1, row(ln_g[i, 0]), row(ln_b[i, 0]), alpha)
        xs = _ffn_ln(xs, w_in1, w_out1, row(ln_g[i, 0]), row(ln_b[i, 0]), alpha)

        w_in2, w_out2 = ffn_weights(1)
        wg = ple_w_gate[i].astype(BF16)
        wp = ple_w_proj[i].astype(BF16)
        bg = row(ple_b_gate[i])
        g3, b3 = ln_g[i, 1:4], ln_b[i, 1:4]
        tail = functools.partial(_tail, w_in=w_in2, w_out=w_out2, layer=i, wg=wg, bg=bg, wp=wp,
                                 g3=g3, b3=b3, alpha=alpha)

        if i % 2 == 0:
            w = fox_w_in[j]
            wq, wk, wv = (w[:, c * d:(c + 1) * d].astype(BF16) for c in range(3))
            reps = LANES // n_heads
            wf = jnp.tile(w[:, 3 * d:], (1, reps)).astype(BF16)
            bf = jnp.tile(fox_b_f[j], reps).reshape(1, LANES)
            wo = fox_w_o[j].astype(BF16)

            wt = jnp.transpose(w)[:3 * d].astype(BF16)
            qxt, kx, vxt, kt_all, vt_all, ft = _fox_proj_prompt(
                xp, wt, wk, wf, bf, batch, seq, q_scale, n_heads, j, n_fox, kv_all)
            kv_all = (kt_all, vt_all)
            fp.append(ft)
            o = _fox_attn_prompt(kx, qxt, vxt, batch, seq)
            xp = tail(xp, "fox", (o, wo), p_all=p_prompt_rows)

            q2, k2, v2, f2 = _fox_proj_sample(xs, wq, wk, wv, wf, bf, q_scale, n_heads)
            keys = -(-(past + t_new) // LANES) * LANES
            lft = jnp.concatenate(
                [cache_fox_logf[j].astype(F32), f2.reshape(nb, t_new, n_heads),
                 jnp.zeros((nb, keys - past - t_new, n_heads), F32)], axis=1).transpose(0, 2, 1)
            o2 = _fox_attn_sample(q2.reshape(nb, t_new, d), k2.reshape(nb, t_new, d),
                                  v2.reshape(nb, t_new, d), cache_kt, cache_vt, j, lft, n_heads)
            xs = tail(xs, "fox", (o2.reshape(nb * t_new, d), wo), p_all=p_sample_rows)
            ksm.append(k2.reshape(nb, t_new, n_heads, HEAD_DIM))
            vsm.append(v2.reshape(nb, t_new, n_heads, HEAD_DIM))
            fsm.append(f2.reshape(nb, t_new, n_heads))
        else:
            wpool = pool_w[j].astype(BF16)
            sc = row(pool_scale[j])
            poolp.append(xp.reshape(batch, seq, d)[:, seq - n_state:])
            xp = tail(xp, "pool", (wpool, sc), p_all=p_prompt_rows, seq_len=seq)

            xs3 = xs.reshape(nb, t_new, d)
            ext = jnp.concatenate([state_pool[j], xs3.astype(state_pool.dtype)], axis=1)
            pools.append(ext[:, -n_state:])
            seg = POOL_HALO + t_new
            ext = jnp.concatenate([jnp.zeros((nb, seg - n_state - t_new, d), F32), ext], axis=1)
            ys = _pool_ln(ext.reshape(nb * seg, d), wpool, sc, row(g3[0]), row(b3[0]), alpha, seg)
            xs = ys.reshape(nb, seg, d)[:, seg - t_new:].reshape(nb * t_new, d)
            xs = tail(xs, "none", (), p_all=p_sample_rows)

    def token_major(all_layers):
        return jnp.transpose(all_layers.reshape(-1, batch, n_heads, HEAD_DIM, seq), (0, 1, 4, 2, 3))

    return (xp.reshape(batch, seq, d), xs.reshape(nb, t_new, d),
            token_major(kv_all[0]), token_major(kv_all[1]),
            jnp.transpose(jnp.stack(fp), (0, 1, 3, 2)), jnp.stack(poolp),
            jnp.stack(ksm), jnp.stack(vsm), jnp.stack(fsm), jnp.stack(pools))
```

```python
import functools

import numpy as np
import jax
import jax.numpy as jnp
from jax import lax
from jax.experimental import pallas as pl
from jax.experimental.pallas import tpu as pltpu

F32 = jnp.float32
BF16 = jnp.bfloat16

LANES = 128
HEAD_DIM = 64
POOL_WINDOWS = (2, 4, 8, 16)
POOL_HALO = 16
LN_EPS = 1e-5
LOG2E = 1.4426950408889634
NEG_BIG = -1e30
PRUNE_LOG2 = 160.0
NORM_MARGIN = 1.02
VMEM_LIMIT = 56 * 1024 * 1024

ROW_TILE = 512
FFN_CHUNK = 256
ATTN_TILE = 512


def _params(n_axes=1):
    return pltpu.CompilerParams(
        dimension_semantics=("arbitrary",) * n_axes, vmem_limit_bytes=VMEM_LIMIT)


def _resident(shape):
    nd = len(shape)
    return pl.BlockSpec(shape, lambda *_: (0,) * nd, pipeline_mode=pl.Buffered(1))


def _rows(tm, width):
    return pl.BlockSpec((tm, width), lambda i: (i, 0))


def _sigmoid(z):
    return 1.0 / (1.0 + jnp.exp(-z))


def _post_norm(x, sub, g, b, alpha):
    y = alpha * x + sub
    mu = jnp.mean(y, axis=-1, keepdims=True)
    yc = y - mu
    var = jnp.mean(yc * yc, axis=-1, keepdims=True)
    return yc * lax.rsqrt(var + LN_EPS) * g + b


def _split3(x):
    hi = x.astype(BF16)
    r1 = x - hi.astype(F32)
    mid = r1.astype(BF16)
    lo = (r1 - mid.astype(F32)).astype(BF16)
    return hi, mid, lo


def _dot(a, b):
    return jnp.dot(a, b, preferred_element_type=F32)


def _dot_nt(a, b):
    return lax.dot_general(a, b, (((1,), (1,)), ((), ())), preferred_element_type=F32)


def _swiglu(x, win_ref, wout_ref):
    xb = x.astype(BF16)
    d_ff = wout_ref.shape[0]
    acc = None
    for c0 in range(0, d_ff, FFN_CHUNK):
        a = _dot(xb, win_ref[:, c0:c0 + FFN_CHUNK])
        u = _dot(xb, win_ref[:, d_ff + c0:d_ff + c0 + FFN_CHUNK])
        act = (a * _sigmoid(a) * u).astype(BF16)
        part = _dot(act, wout_ref[c0:c0 + FFN_CHUNK, :])
        acc = part if acc is None else acc + part
    return acc


def _ffn_ln_kernel(x_ref, win_ref, wout_ref, g_ref, b_ref, o_ref, *, alpha):
    x = x_ref[...]
    o_ref[...] = _post_norm(x, 0.5 * _swiglu(x, win_ref, wout_ref), g_ref[...], b_ref[...], alpha)


def _ffn_ln(x, w_in, w_out, g, b, alpha):
    rows, d = x.shape
    tm = min(ROW_TILE, rows)
    return pl.pallas_call(
        functools.partial(_ffn_ln_kernel, alpha=alpha),
        grid=(rows // tm,),
        in_specs=[_rows(tm, d), _resident(w_in.shape), _resident(w_out.shape),
                  _resident(g.shape), _resident(b.shape)],
        out_specs=_rows(tm, d),
        out_shape=jax.ShapeDtypeStruct((rows, d), F32),
        compiler_params=_params(),
        name="ffn_ln",
    )(x, w_in, w_out, g, b)


def _pool_mix(x, halo_ref, w_ref, scale_ref, ext_sc, seq_len):
    tm, d = x.shape
    start = (pl.program_id(0) * tm) % seq_len
    grp = d // len(POOL_WINDOWS)
    ext_sc[0:POOL_HALO, :] = jnp.where(start == 0, 0.0, halo_ref[...])
    ext_sc[POOL_HALO:, :] = x
    pos = (start + lax.broadcasted_iota(jnp.int32, (tm, 1), 0)) % seq_len
    outs = []
    for gi, w in enumerate(POOL_WINDOWS):
        sl = slice(gi * grp, (gi + 1) * grp)
        tot = x[:, sl]
        for back in range(1, w):
            tot = tot + ext_sc[POOL_HALO - back:POOL_HALO - back + tm, sl]
        cnt = jnp.minimum(pos + 1, w).astype(F32)
        pooled = tot / cnt - x[:, sl]
        outs.append(_dot(pooled.astype(BF16), w_ref[gi]))
    return jnp.concatenate(outs, axis=1) * scale_ref[...]


def _pool_halo_spec(tm, d):
    halo_blocks = tm // POOL_HALO
    return pl.BlockSpec((POOL_HALO, d), lambda i: (jnp.maximum(i * halo_blocks - 1, 0), 0))


def _pool_ln_kernel(x_ref, halo_ref, w_ref, scale_ref, g_ref, b_ref, o_ref, ext_sc, *, alpha, seq_len):
    x = x_ref[...]
    y = _pool_mix(x, halo_ref, w_ref, scale_ref, ext_sc, seq_len)
    o_ref[...] = _post_norm(x, y, g_ref[...], b_ref[...], alpha)


def _pool_ln(x, w_pool, scale, g, b, alpha, seq_len):
    rows, d = x.shape
    tm = min(ROW_TILE, rows)
    assert seq_len % tm == 0 or tm % seq_len == 0
    return pl.pallas_call(
        functools.partial(_pool_ln_kernel, alpha=alpha, seq_len=seq_len),
        grid=(rows // tm,),
        in_specs=[_rows(tm, d), _pool_halo_spec(tm, d), _resident(w_pool.shape), _resident(scale.shape),
                  _resident(g.shape), _resident(b.shape)],
        out_specs=_rows(tm, d),
        out_shape=jax.ShapeDtypeStruct((rows, d), F32),
        scratch_shapes=[pltpu.VMEM((POOL_HALO + tm, d), F32)],
        compiler_params=_params(),
        name="pool_ln",
    )(x, x, w_pool, scale, g, b)


def _tail_kernel(*refs, mixer, alpha, seq_len):
    x_ref, refs = refs[0], refs[1:]
    n_mix = {"fox": 2, "pool": 3, "none": 0}[mixer]
    mix_refs, refs = refs[:n_mix], refs[n_mix:]
    win_ref, wout_ref, p_ref, wg_ref, bg_ref, wp_ref, g_ref, b_ref, o_ref = refs[:9]
    x = x_ref[...]
    if mixer == "fox":
        attn_ref, wo_ref = mix_refs
        x = _post_norm(x, _dot(attn_ref[...], wo_ref[...]), g_ref[0:1], b_ref[0:1], alpha)
    elif mixer == "pool":
        halo_ref, wpool_ref, scale_ref = mix_refs
        y = _pool_mix(x, halo_ref, wpool_ref, scale_ref, refs[9], seq_len)
        x = _post_norm(x, y, g_ref[0:1], b_ref[0:1], alpha)
    x = _post_norm(x, 0.5 * _swiglu(x, win_ref, wout_ref), g_ref[1:2], b_ref[1:2], alpha)
    gate = _sigmoid(_dot(x.astype(BF16), wg_ref[...]) + bg_ref[...])
    proj = _dot(p_ref[...].astype(BF16), wp_ref[...])
    o_ref[...] = _post_norm(x, proj * gate, g_ref[2:3], b_ref[2:3], alpha)


def _tail(x, mixer, mix_args, w_in, w_out, p_all, layer, wg, bg, wp, g3, b3, alpha, seq_len=None):
    rows, d = x.shape
    tm = min(ROW_TILE, rows)
    tiles = rows // tm
    if mixer == "fox":
        attn, wo = mix_args
        mix_specs, mix_ops = [_rows(tm, d), _resident(wo.shape)], [attn, wo]
    elif mixer == "pool":
        w_pool, scale = mix_args
        mix_specs = [_pool_halo_spec(tm, d), _resident(w_pool.shape), _resident(scale.shape)]
        mix_ops = [x, w_pool, scale]
    else:
        mix_specs, mix_ops = [], []
    return pl.pallas_call(
        functools.partial(_tail_kernel, mixer=mixer, alpha=alpha, seq_len=seq_len),
        grid=(tiles,),
        in_specs=[_rows(tm, d)] + mix_specs + [
            _resident(w_in.shape), _resident(w_out.shape),
            pl.BlockSpec((tm, p_all.shape[1]), lambda i: (layer * tiles + i, 0)),
            _resident(wg.shape), _resident(bg.shape), _resident(wp.shape),
            _resident(g3.shape), _resident(b3.shape)],
        out_specs=_rows(tm, d),
        out_shape=jax.ShapeDtypeStruct((rows, d), F32),
        scratch_shapes=[pltpu.VMEM((POOL_HALO + tm, d), F32)] if mixer == "pool" else [],
        compiler_params=_params(),
        name="tail_" + mixer,
    )(x, *mix_ops, w_in, w_out, p_all, wg, bg, wp, g3, b3)


def _log_sigmoid(z):
    return jnp.minimum(z, 0.0) - jnp.log1p(jnp.exp(-jnp.abs(z)))


def _lane_group_select(a, b, c):
    lane = lax.broadcasted_iota(jnp.int32, a.shape, 1)
    a, b, c = a.astype(F32), b.astype(F32), c.astype(F32)
    sel = jnp.where(lane < 16, a, jnp.where(lane < 32, b, jnp.where(
        lane < 48, c, jnp.where(lane < 64, 1.0, 0.0))))
    return sel.astype(BF16)


def _expand_heads(t):
    n = t.shape[1] // LANES
    return jnp.concatenate(
        [t[:, (gidx // 2) * LANES:(gidx // 2 + 1) * LANES] for gidx in range(2 * n)], axis=1)


def _fox_proj_prompt_kernel(x_ref, wt_ref, wf_ref, bf_ref, tri_ref, pk_ref, psel_ref, *rest,
                            q_scale, n_heads, first_layer):
    qxt_ref, kx_ref, vxt_ref, kt_ref, vt_ref, logft_ref, stats_ref, carry_sc = rest[-8:]

    @pl.when(pl.program_id(1) == 0)
    def _():
        carry_sc[...] = jnp.zeros_like(carry_sc)

    xb = x_ref[...].astype(BF16)
    tm, d = xb.shape
    ht = _dot_nt(wt_ref[...], xb)
    kt_ref[0, 0] = ht[d:2 * d]
    vt_ref[0, 0] = ht[2 * d:3 * d]
    if first_layer:
        for later in range(1, kt_ref.shape[0]):
            kt_ref[later, 0] = jnp.zeros((d, tm), F32)
            vt_ref[later, 0] = jnp.zeros((d, tm), F32)
    k_tok = ht[d:2 * d].T

    logf = _log_sigmoid(_dot(xb, wf_ref[...]) + bf_ref[...])
    logft_ref[0] = logf.T[0:n_heads]

    hi, mid, lo = _split3(logf)
    cs = _dot(tri_ref[...], jnp.concatenate([hi, mid, lo], axis=1))
    cum = (cs[:, :LANES] + cs[:, LANES:2 * LANES] + cs[:, 2 * LANES:]) + carry_sc[...]
    carry_sc[...] = cum[tm - 1:tm, :]

    f2 = cum * LOG2E
    fh, fm, fl = _split3(f2)
    ak = _dot(_lane_group_select(fh, fm, fl), pk_ref[...])
    lane = lax.broadcasted_iota(jnp.int32, ak.shape, 1)
    own = (((lane >> 6) ^ (lane >> 7)) & 1) == 0
    kx_ref[...] = jnp.where(own, _expand_heads(k_tok), ak).astype(BF16)

    f2t = f2.T[0:n_heads]
    th, tmid, tl = _split3(f2t)
    pieces = jnp.concatenate([th, tmid, tl, jnp.ones_like(th)], axis=0)
    aq = _dot(psel_ref[...], pieces).astype(BF16)
    qt = (ht[0:d] * q_scale).astype(BF16)
    vt = ht[2 * d:3 * d].astype(BF16)

    def head_max_norm(t):
        sq = (t * t).reshape(n_heads, HEAD_DIM, tm)
        return jnp.sqrt(jnp.max(jnp.sum(sq, axis=1), axis=1, keepdims=True))

    stats = [f2t[:, 0:1], f2t[:, tm - 1:tm], head_max_norm(qt.astype(F32)), head_max_norm(ht[d:2 * d])]
    stats_ref[0, 0] = jnp.concatenate(
        [jnp.broadcast_to(col, (n_heads, LANES)) for col in stats], axis=0)
    sub = lax.broadcasted_iota(jnp.int32, (16, tm), 0)
    ones_row = jnp.where(sub == 0, 1.0, 0.0).astype(BF16)
    zeros = jnp.zeros((HEAD_DIM - 16, tm), BF16)
    for h in range(n_heads):
        q_h = qt[h * HEAD_DIM:(h + 1) * HEAD_DIM]
        a_h = aq[16 * h:16 * (h + 1)]
        parts = [q_h, a_h, zeros] if h % 2 == 0 else [a_h, zeros, q_h]
        qxt_ref[0, 0, h * LANES:(h + 1) * LANES, :] = jnp.concatenate(parts, axis=0)
        vxt_ref[0, 0, h * LANES:(h + 1) * LANES, :] = jnp.concatenate(
            [vt[h * HEAD_DIM:(h + 1) * HEAD_DIM], ones_row, zeros], axis=0)


def _aug_placement(n_heads):
    pk = np.zeros((LANES, 2 * n_heads * HEAD_DIM), np.float32)
    psel = np.zeros((16 * n_heads, 4 * n_heads), np.float32)
    for h in range(n_heads):
        base = h * LANES + (HEAD_DIM if h % 2 == 0 else 0)
        for piece in range(3):
            pk[3 * n_heads + h, base + piece] = 1.0
            pk[n_heads * piece + h, base + 3 + piece] = -1.0
            psel[16 * h + piece, n_heads * piece + h] = 1.0
            psel[16 * h + 3 + piece, 3 * n_heads + h] = 1.0
    return jnp.asarray(pk, BF16), jnp.asarray(psel, BF16)


def _fox_proj_prompt(x, wt, wf, bf, batch, seq_len, q_scale, n_heads, layer, n_layers, kv_all):
    rows, d = x.shape
    tm = min(ATTN_TILE, seq_len)
    nt = seq_len // tm
    tri = jnp.asarray(np.tril(np.ones((tm, tm), np.float32)), BF16)
    pk, psel = _aug_placement(n_heads)
    fm_tiles = pl.BlockSpec((1, 1, 2 * d, tm), lambda b, t: (b, t, 0, 0))
    first = kv_all is None
    assert first == (layer == 0)
    kv_spec = pl.BlockSpec((n_layers if first else 1, 1, d, tm), lambda b, t: (layer, b, 0, t))
    kv_shape = jax.ShapeDtypeStruct((n_layers, batch, d, seq_len), F32)
    n_in = 7
    return pl.pallas_call(
        functools.partial(_fox_proj_prompt_kernel, q_scale=q_scale, n_heads=n_heads, first_layer=first),
        grid=(batch, nt),
        in_specs=[pl.BlockSpec((tm, d), lambda b, t: (b * nt + t, 0)), _resident(wt.shape),
                  _resident(wf.shape), _resident(bf.shape), _resident(tri.shape),
                  _resident(pk.shape), _resident(psel.shape)]
                 + ([] if first else [pl.BlockSpec(memory_space=pl.ANY)] * 2),
        out_specs=[fm_tiles, pl.BlockSpec((tm, 2 * d), lambda b, t: (b * nt + t, 0)), fm_tiles,
                   kv_spec, kv_spec, pl.BlockSpec((1, n_heads, tm), lambda b, t: (b, 0, t)),
                   pl.BlockSpec((1, 1, 4 * n_heads, LANES), lambda b, t: (b, t, 0, 0))],
        out_shape=[jax.ShapeDtypeStruct((batch, nt, 2 * d, tm), BF16),
                   jax.ShapeDtypeStruct((rows, 2 * d), BF16),
                   jax.ShapeDtypeStruct((batch, nt, 2 * d, tm), BF16),
                   kv_shape, kv_shape,
                   jax.ShapeDtypeStruct((batch, n_heads, seq_len), F32),
                   jax.ShapeDtypeStruct((batch, nt, 4 * n_heads, LANES), F32)],
        input_output_aliases={} if first else {n_in: 3, n_in + 1: 4},
        scratch_shapes=[pltpu.VMEM((1, LANES), F32)],
        compiler_params=_params(2),
        name="fox_proj_prompt",
    )(x, wt, wf, bf, tri, pk, psel, *([] if first else kv_all))


def _fox_proj_sample_kernel(x_ref, wq_ref, wk_ref, wv_ref, wf_ref, bf_ref,
                            q_ref, k_ref, v_ref, logf_ref, *, q_scale, n_heads):
    xb = x_ref[...].astype(BF16)
    q_ref[...] = (_dot(xb, wq_ref[...]) * q_scale).astype(BF16)
    k_ref[...] = _dot(xb, wk_ref[...])
    v_ref[...] = _dot(xb, wv_ref[...])
    logf = _log_sigmoid(_dot(xb, wf_ref[...]) + bf_ref[...])
    logf_ref[...] = logf[:, :n_heads]


def _fox_proj_sample(x, wq, wk, wv, wf, bf, q_scale, n_heads):
    rows, d = x.shape
    tm = min(ROW_TILE, rows)
    return pl.pallas_call(
        functools.partial(_fox_proj_sample_kernel, q_scale=q_scale, n_heads=n_heads),
        grid=(rows // tm,),
        in_specs=[_rows(tm, d), _resident(wq.shape), _resident(wk.shape), _resident(wv.shape),
                  _resident(wf.shape), _resident(bf.shape)],
        out_specs=[_rows(tm, d), _rows(tm, d), _rows(tm, d), _rows(tm, n_heads)],
        out_shape=[jax.ShapeDtypeStruct((rows, d), BF16), jax.ShapeDtypeStruct((rows, d), F32),
                   jax.ShapeDtypeStruct((rows, d), F32), jax.ShapeDtypeStruct((rows, n_heads), F32)],
        compiler_params=_params(),
        name="fox_proj_sample",
    )(x, wq, wk, wv, wf, bf)


def _fox_attn_prompt_kernel(stats_ref, kx_ref, qxt_ref, vxt_ref, o_ref, *, tile, n_heads):
    n_blocks = qxt_ref.shape[1]
    key_in_block = lax.broadcasted_iota(jnp.int32, (tile, tile), 0)
    query_in_block = lax.broadcasted_iota(jnp.int32, (tile, tile), 1)
    causal = key_in_block <= query_in_block
    groups = [slice(h * LANES, (h + 1) * LANES) for h in range(2)]

    def scores(qi, kj, masked):
        k0 = pl.multiple_of(kj * tile, tile)
        out = []
        for h in range(2):
            s_t = _dot(kx_ref[pl.ds(k0, tile), groups[h]], qxt_ref[0, qi, groups[h], :])
            if masked:
                s_t = jnp.where(causal, s_t, -jnp.inf)
            out.append((s_t, jnp.max(s_t, axis=0, keepdims=True)))
        return tuple(out)

    def accumulate(kj, sc, state):
        new_state = []
        for h in range(2):
            m_prev, acc = state[h]
            s_t, m_blk = sc[h]
            m_new = jnp.maximum(m_prev, m_blk)
            alpha = jnp.exp2(m_prev - m_new)
            p_t = jnp.exp2(s_t - m_new).astype(BF16)
            acc = alpha * acc + _dot(vxt_ref[0, kj, groups[h], :], p_t)
            new_state.append((m_new, acc))
        return tuple(new_state)

    def finish(qi, state):
        o_t = jnp.concatenate(
            [acc[0:HEAD_DIM] / acc[HEAD_DIM:HEAD_DIM + 1] for _, acc in state], axis=0)
        o_ref[pl.ds(pl.multiple_of(qi * tile, tile), tile), :] = o_t.T.astype(o_ref.dtype)

    init = tuple((jnp.full((1, tile), NEG_BIG, F32), jnp.zeros((LANES, tile), F32)) for _ in range(2))

    def merge(a, b):
        out = []
        for (m_a, acc_a), (m_b, acc_b) in zip(a, b):
            m = jnp.maximum(m_a, m_b)
            out.append((m, jnp.exp2(m_a - m) * acc_a + jnp.exp2(m_b - m) * acc_b))
        return tuple(out)

    batch_idx, pair_idx = pl.program_id(0), pl.program_id(1)

    def stat(kind, h, blk):
        return stats_ref[((batch_idx * 4 + kind) * n_heads + 2 * pair_idx + h) * n_blocks + blk]

    kn_max = [lax.fori_loop(0, n_blocks, lambda blk, mx: jnp.maximum(mx, stat(3, h, blk)), 0.0)
              for h in range(2)]

    def first_needed_block(qi):
        first = qi
        for h in range(2):
            slack = NORM_MARGIN * stat(2, h, qi) * (kn_max[h] + stat(3, h, qi)) + PRUNE_LOG2
            f_q = stat(0, h, qi)
            first = jnp.minimum(first, lax.while_loop(
                lambda j: jnp.logical_and(j < qi, f_q - stat(1, h, j) + slack <= 0.0),
                lambda j: j + 1, 0))
        return first

    def q_block(qi, carry):
        def pair(jp, both):
            sc_a = scores(qi, 2 * jp, False)
            sc_b = scores(qi, 2 * jp + 1, False)
            return accumulate(2 * jp, sc_a, both[0]), accumulate(2 * jp + 1, sc_b, both[1])

        st_a, st_b = lax.fori_loop(first_needed_block(qi) // 2, qi // 2, pair, (init, init))
        st_b = lax.fori_loop(
            0, qi % 2, lambda _, st: accumulate(qi - 1, scores(qi, qi - 1, False), st), st_b)
        st_a = accumulate(qi, scores(qi, qi, True), st_a)
        finish(qi, merge(st_a, st_b))
        return carry

    lax.fori_loop(0, n_blocks, q_block, 0)


def _fox_attn_prompt(stats, kx, qxt, vxt, batch, seq_len, n_heads):
    rows, d2 = kx.shape
    d = d2 // 2
    pairs = d // LANES
    tile = qxt.shape[3]
    n_blocks = seq_len // tile
    table = stats[:, :, :, 0].reshape(batch, n_blocks, 4, n_heads).transpose(0, 2, 3, 1).reshape(-1)
    feature_major = pl.BlockSpec((1, n_blocks, 2 * LANES, tile), lambda b, p: (b, 0, p, 0))
    return pl.pallas_call(
        functools.partial(_fox_attn_prompt_kernel, tile=tile, n_heads=n_heads),
        grid=(batch, pairs),
        in_specs=[pl.BlockSpec(memory_space=pltpu.SMEM),
                  pl.BlockSpec((seq_len, 2 * LANES), lambda b, p: (b, p)), feature_major, feature_major],
        out_specs=pl.BlockSpec((seq_len, LANES), lambda b, p: (b, p)),
        out_shape=jax.ShapeDtypeStruct((rows, d), BF16),
        compiler_params=_params(2),
        name="fox_attn_prompt",
    )(table, kx, qxt, vxt)


def _fox_attn_sample_kernel(q_ref, kn_ref, vn_ref, ckt_ref, cvt_ref, lft_ref, o_ref, *, n_heads):
    t_new, d = q_ref.shape[1], q_ref.shape[2]
    past = ckt_ref.shape[3]
    keys = lft_ref.shape[2]
    qrows = n_heads * t_new

    cum = lft_ref[0]
    key_idx = lax.broadcasted_iota(jnp.int32, cum.shape, 1)
    step = 1
    while step < keys:
        cum = cum + jnp.where(key_idx >= step, pltpu.roll(cum, step, axis=1), 0.0)
        step *= 2

    r_head = lax.broadcasted_iota(jnp.int32, (qrows, n_heads), 0) // t_new
    rep_head = jnp.where(r_head == lax.broadcasted_iota(jnp.int32, (qrows, n_heads), 1),
                         1.0, 0.0).astype(BF16)
    r_query = lax.broadcasted_iota(jnp.int32, (qrows, t_new), 0) % t_new
    rep_query = jnp.where(r_query == lax.broadcasted_iota(jnp.int32, (qrows, t_new), 1),
                          1.0, 0.0).astype(BF16)

    c_hi, c_mid, c_lo = _split3(cum * LOG2E)
    cum_rows = _dot(rep_head, c_hi) + _dot(rep_head, c_mid) + _dot(rep_head, c_lo)
    kidx = lax.broadcasted_iota(jnp.int32, (qrows, keys), 1)
    qpos = past + lax.broadcasted_iota(jnp.int32, (qrows, keys), 0) % t_new
    cum_q = jnp.sum(jnp.where(kidx == qpos, cum_rows, 0.0), axis=1, keepdims=True)

    lane_head = lax.broadcasted_iota(jnp.int32, (qrows, d), 1) // HEAD_DIM
    row_head = lax.broadcasted_iota(jnp.int32, (qrows, d), 0) // t_new
    q_rep = _dot(rep_query, q_ref[0])
    q_exp = jnp.where(lane_head == row_head, q_rep, 0.0).astype(BF16)

    bias = cum_q - cum_rows
    s_old = _dot(q_exp, ckt_ref[0, 0].astype(BF16)) + bias[:, 0:past]
    s_new = _dot_nt(q_exp, kn_ref[0].astype(BF16)) + bias[:, past:past + t_new]
    new_key = lax.broadcasted_iota(jnp.int32, (qrows, t_new), 1)
    new_query = lax.broadcasted_iota(jnp.int32, (qrows, t_new), 0) % t_new
    s_new = jnp.where(new_key <= new_query, s_new, -jnp.inf)
    m = jnp.maximum(jnp.max(s_old, axis=1, keepdims=True), jnp.max(s_new, axis=1, keepdims=True))
    p_old = jnp.exp2(s_old - m)
    p_new = jnp.exp2(s_new - m)
    l = jnp.sum(p_old, axis=1, keepdims=True) + jnp.sum(p_new, axis=1, keepdims=True)
    o_all = (_dot_nt(p_old.astype(BF16), cvt_ref[0, 0].astype(BF16))
             + _dot(p_new.astype(BF16), vn_ref[0].astype(BF16))) / l
    o_all = jnp.where(lane_head == row_head, o_all, 0.0)
    out = o_all[0:t_new, :]
    for h in range(1, n_heads):
        out = out + o_all[h * t_new:(h + 1) * t_new, :]
    o_ref[0] = out.astype(o_ref.dtype)


def _fox_attn_sample(q, k_new, v_new, cache_kt, cache_vt, layer, lft, n_heads):
    nb, t_new, d = q.shape
    past = cache_kt.shape[3]
    keys = lft.shape[2]
    per_stream = lambda shape: pl.BlockSpec((1,) + shape, lambda b: (b, 0, 0))
    cache = pl.BlockSpec((1, 1, d, past), lambda b: (layer, b, 0, 0))
    return pl.pallas_call(
        functools.partial(_fox_attn_sample_kernel, n_heads=n_heads),
        grid=(nb,),
        in_specs=[per_stream((t_new, d)), per_stream((t_new, d)), per_stream((t_new, d)),
                  cache, cache, per_stream((n_heads, keys))],
        out_specs=per_stream((t_new, d)),
        out_shape=jax.ShapeDtypeStruct((nb, t_new, d), BF16),
        compiler_params=_params(),
        name="fox_attn_sample",
    )(q, k_new, v_new, cache_kt, cache_vt, lft)


def kernel(x_prompt, x_sample, cache_fox_k, cache_fox_v, cache_fox_logf, state_pool, p_prompt, p_sample,
           ln_g, ln_b, ffn_w_in, ffn_w_out, fox_w_in, fox_b_f, fox_w_o, pool_w, pool_scale,
           ple_w_proj, ple_w_gate, ple_b_gate):
    batch, seq, d = x_prompt.shape
    nb, t_new, _ = x_sample.shape
    depth = ln_g.shape[0]
    n_heads = d // HEAD_DIM
    d_ff = ffn_w_out.shape[2]
    past = cache_fox_k.shape[2]
    n_state = state_pool.shape[2]
    alpha = (2.0 * depth) ** 0.25
    q_scale = HEAD_DIM ** -0.5 * LOG2E
    assert d_ff % FFN_CHUNK == 0

    xp = x_prompt.reshape(batch * seq, d)
    xs = x_sample.reshape(nb * t_new, d)
    row = lambda a: a.reshape(1, -1)
    p_prompt_rows = p_prompt.reshape(depth * batch * seq, -1)
    p_sample_rows = p_sample.reshape(depth * nb * t_new, -1)
    cache_kt = jnp.transpose(cache_fox_k, (0, 1, 3, 4, 2)).reshape(-1, nb, d, past)
    cache_vt = jnp.transpose(cache_fox_v, (0, 1, 3, 4, 2)).reshape(-1, nb, d, past)

    n_fox = (depth + 1) // 2
    kv_all = None
    fp, poolp, ksm, vsm, fsm, pools = [], [], [], [], [], []
    for i in range(depth):
        j = i // 2

        def ffn_weights(s):
            return ffn_w_in[i, s].astype(BF16), ffn_w_out[i, s].astype(BF16)

        w_in1, w_out1 = ffn_weights(0)
        xp = _ffn_ln(xp, w_in1, w_out1, row(ln_g[i, 0]), row(ln_b[i, 0]), alpha)
        xs = _ffn_ln(xs, w_in1, w_out1, row(ln_g[i, 0]), row(ln_b[i, 0]), alpha)

        w_in2, w_out2 = ffn_weights(1)
        wg = ple_w_gate[i].astype(BF16)
        wp = ple_w_proj[i].astype(BF16)
        bg = row(ple_b_gate[i])
        g3, b3 = ln_g[i, 1:4], ln_b[i, 1:4]
        tail = functools.partial(_tail, w_in=w_in2, w_out=w_out2, layer=i, wg=wg, bg=bg, wp=wp,
                                 g3=g3, b3=b3, alpha=alpha)

        if i % 2 == 0:
            w = fox_w_in[j]
            wq, wk, wv = (w[:, c * d:(c + 1) * d].astype(BF16) for c in range(3))
            reps = LANES // n_heads
            wf = jnp.tile(w[:, 3 * d:], (1, reps)).astype(BF16)
            bf = jnp.tile(fox_b_f[j], reps).reshape(1, LANES)
            wo = fox_w_o[j].astype(BF16)

            wt = jnp.transpose(w)[:3 * d].astype(BF16)
            qxt, kx, vxt, kt_all, vt_all, ft, stats = _fox_proj_prompt(
                xp, wt, wf, bf, batch, seq, q_scale, n_heads, j, n_fox, kv_all)
            kv_all = (kt_all, vt_all)
            fp.append(ft)
            o = _fox_attn_prompt(stats, kx, qxt, vxt, batch, seq, n_heads)
            xp = tail(xp, "fox", (o, wo), p_all=p_prompt_rows)

            q2, k2, v2, f2 = _fox_proj_sample(xs, wq, wk, wv, wf, bf, q_scale, n_heads)
            keys = -(-(past + t_new) // LANES) * LANES
            lft = jnp.concatenate(
                [cache_fox_logf[j].astype(F32), f2.reshape(nb, t_new, n_heads),
                 jnp.zeros((nb, keys - past - t_new, n_heads), F32)], axis=1).transpose(0, 2, 1)
            o2 = _fox_attn_sample(q2.reshape(nb, t_new, d), k2.reshape(nb, t_new, d),
                                  v2.reshape(nb, t_new, d), cache_kt, cache_vt, j, lft, n_heads)
            xs = tail(xs, "fox", (o2.reshape(nb * t_new, d), wo), p_all=p_sample_rows)
            ksm.append(k2.reshape(nb, t_new, n_heads, HEAD_DIM))
            vsm.append(v2.reshape(nb, t_new, n_heads, HEAD_DIM))
            fsm.append(f2.reshape(nb, t_new, n_heads))
        else:
            wpool = pool_w[j].astype(BF16)
            sc = row(pool_scale[j])
            poolp.append(xp.reshape(batch, seq, d)[:, seq - n_state:])
            xp = tail(xp, "pool", (wpool, sc), p_all=p_prompt_rows, seq_len=seq)

            xs3 = xs.reshape(nb, t_new, d)
            ext = jnp.concatenate([state_pool[j], xs3.astype(state_pool.dtype)], axis=1)
            pools.append(ext[:, -n_state:])
            seg = POOL_HALO + t_new
            ext = jnp.concatenate([jnp.zeros((nb, seg - n_state - t_new, d), F32), ext], axis=1)
            ys = _pool_ln(ext.reshape(nb * seg, d), wpool, sc, row(g3[0]), row(b3[0]), alpha, seg)
            xs = ys.reshape(nb, seg, d)[:, seg - t_new:].reshape(nb * t_new, d)
            xs = tail(xs, "none", (), p_all=p_sample_rows)

    def token_major(all_layers):
        return jnp.transpose(all_layers.reshape(-1, batch, n_heads, HEAD_DIM, seq), (0, 1, 4, 2, 3))

    return (xp.reshape(batch, seq, d), xs.reshape(nb, t_new, d),
            token_major(kv_all[0]), token_major(kv_all[1]),
            jnp.transpose(jnp.stack(fp), (0, 1, 3, 2)), jnp.stack(poolp),
            jnp.stack(ksm), jnp.stack(vsm), jnp.stack(fsm), jnp.stack(pools))
```

```python
import functools

import numpy as np
import jax
import jax.numpy as jnp
from jax import lax
from jax.experimental import pallas as pl
from jax.experimental.pallas import tpu as pltpu

F32 = jnp.float32
BF16 = jnp.bfloat16

LANES = 128
HEAD_DIM = 64
POOL_WINDOWS = (2, 4, 8, 16)
POOL_HALO = 16
LN_EPS = 1e-5
LOG2E = 1.4426950408889634
NEG_BIG = -1e30
V_ROWS = HEAD_DIM + 16
PRUNE_LOG2 = 160.0
NORM_MARGIN = 1.02
VMEM_LIMIT = 56 * 1024 * 1024

ROW_TILE = 512
FFN_CHUNK = 256
ATTN_TILE = 512


def _params(n_axes=1):
    return pltpu.CompilerParams(
        dimension_semantics=("arbitrary",) * n_axes, vmem_limit_bytes=VMEM_LIMIT)


def _resident(shape):
    nd = len(shape)
    return pl.BlockSpec(shape, lambda *_: (0,) * nd, pipeline_mode=pl.Buffered(1))


def _rows(tm, width):
    return pl.BlockSpec((tm, width), lambda i: (i, 0))


def _sigmoid(z):
    return 1.0 / (1.0 + jnp.exp(-z))


def _post_norm(x, sub, g, b, alpha):
    y = alpha * x + sub
    mu = jnp.mean(y, axis=-1, keepdims=True)
    yc = y - mu
    var = jnp.mean(yc * yc, axis=-1, keepdims=True)
    return yc * lax.rsqrt(var + LN_EPS) * g + b


def _split3(x):
    hi = x.astype(BF16)
    r1 = x - hi.astype(F32)
    mid = r1.astype(BF16)
    lo = (r1 - mid.astype(F32)).astype(BF16)
    return hi, mid, lo


def _dot(a, b):
    return jnp.dot(a, b, preferred_element_type=F32)


def _dot_nt(a, b):
    return lax.dot_general(a, b, (((1,), (1,)), ((), ())), preferred_element_type=F32)


def _swiglu(x, win_ref, wout_ref):
    xb = x.astype(BF16)
    d_ff = wout_ref.shape[0]
    acc = None
    for c0 in range(0, d_ff, FFN_CHUNK):
        a = _dot(xb, win_ref[:, c0:c0 + FFN_CHUNK])
        u = _dot(xb, win_ref[:, d_ff + c0:d_ff + c0 + FFN_CHUNK])
        act = (a * _sigmoid(a) * u).astype(BF16)
        part = _dot(act, wout_ref[c0:c0 + FFN_CHUNK, :])
        acc = part if acc is None else acc + part
    return acc


def _ffn_ln_kernel(x_ref, win_ref, wout_ref, g_ref, b_ref, o_ref, *, alpha):
    x = x_ref[...]
    o_ref[...] = _post_norm(x, 0.5 * _swiglu(x, win_ref, wout_ref), g_ref[...], b_ref[...], alpha)


def _ffn_ln(x, w_in, w_out, g, b, alpha):
    rows, d = x.shape
    tm = min(ROW_TILE, rows)
    return pl.pallas_call(
        functools.partial(_ffn_ln_kernel, alpha=alpha),
        grid=(rows // tm,),
        in_specs=[_rows(tm, d), _resident(w_in.shape), _resident(w_out.shape),
                  _resident(g.shape), _resident(b.shape)],
        out_specs=_rows(tm, d),
        out_shape=jax.ShapeDtypeStruct((rows, d), F32),
        compiler_params=_params(),
        name="ffn_ln",
    )(x, w_in, w_out, g, b)


def _pool_mix(x, halo_ref, w_ref, scale_ref, ext_sc, seq_len):
    tm, d = x.shape
    start = (pl.program_id(0) * tm) % seq_len
    grp = d // len(POOL_WINDOWS)
    ext_sc[0:POOL_HALO, :] = jnp.where(start == 0, 0.0, halo_ref[...])
    ext_sc[POOL_HALO:, :] = x
    pos = (start + lax.broadcasted_iota(jnp.int32, (tm, 1), 0)) % seq_len
    outs = []
    for gi, w in enumerate(POOL_WINDOWS):
        sl = slice(gi * grp, (gi + 1) * grp)
        tot = x[:, sl]
        for back in range(1, w):
            tot = tot + ext_sc[POOL_HALO - back:POOL_HALO - back + tm, sl]
        cnt = jnp.minimum(pos + 1, w).astype(F32)
        pooled = tot / cnt - x[:, sl]
        outs.append(_dot(pooled.astype(BF16), w_ref[gi]))
    return jnp.concatenate(outs, axis=1) * scale_ref[...]


def _pool_halo_spec(tm, d):
    halo_blocks = tm // POOL_HALO
    return pl.BlockSpec((POOL_HALO, d), lambda i: (jnp.maximum(i * halo_blocks - 1, 0), 0))


def _pool_ln_kernel(x_ref, halo_ref, w_ref, scale_ref, g_ref, b_ref, o_ref, ext_sc, *, alpha, seq_len):
    x = x_ref[...]
    y = _pool_mix(x, halo_ref, w_ref, scale_ref, ext_sc, seq_len)
    o_ref[...] = _post_norm(x, y, g_ref[...], b_ref[...], alpha)


def _pool_ln(x, w_pool, scale, g, b, alpha, seq_len):
    rows, d = x.shape
    tm = min(ROW_TILE, rows)
    assert seq_len % tm == 0 or tm % seq_len == 0
    return pl.pallas_call(
        functools.partial(_pool_ln_kernel, alpha=alpha, seq_len=seq_len),
        grid=(rows // tm,),
        in_specs=[_rows(tm, d), _pool_halo_spec(tm, d), _resident(w_pool.shape), _resident(scale.shape),
                  _resident(g.shape), _resident(b.shape)],
        out_specs=_rows(tm, d),
        out_shape=jax.ShapeDtypeStruct((rows, d), F32),
        scratch_shapes=[pltpu.VMEM((POOL_HALO + tm, d), F32)],
        compiler_params=_params(),
        name="pool_ln",
    )(x, x, w_pool, scale, g, b)


def _tail_kernel(*refs, mixer, alpha, seq_len):
    x_ref, refs = refs[0], refs[1:]
    n_mix = {"fox": 2, "pool": 3, "none": 0}[mixer]
    mix_refs, refs = refs[:n_mix], refs[n_mix:]
    win_ref, wout_ref, p_ref, wg_ref, bg_ref, wp_ref, g_ref, b_ref, o_ref = refs[:9]
    x = x_ref[...]
    if mixer == "fox":
        attn_ref, wo_ref = mix_refs
        x = _post_norm(x, _dot(attn_ref[...], wo_ref[...]), g_ref[0:1], b_ref[0:1], alpha)
    elif mixer == "pool":
        halo_ref, wpool_ref, scale_ref = mix_refs
        y = _pool_mix(x, halo_ref, wpool_ref, scale_ref, refs[9], seq_len)
        x = _post_norm(x, y, g_ref[0:1], b_ref[0:1], alpha)
    x = _post_norm(x, 0.5 * _swiglu(x, win_ref, wout_ref), g_ref[1:2], b_ref[1:2], alpha)
    gate = _sigmoid(_dot(x.astype(BF16), wg_ref[...]) + bg_ref[...])
    proj = _dot(p_ref[...].astype(BF16), wp_ref[...])
    o_ref[...] = _post_norm(x, proj * gate, g_ref[2:3], b_ref[2:3], alpha)


def _tail(x, mixer, mix_args, w_in, w_out, p_all, layer, wg, bg, wp, g3, b3, alpha, seq_len=None):
    rows, d = x.shape
    tm = min(ROW_TILE, rows)
    tiles = rows // tm
    if mixer == "fox":
        attn, wo = mix_args
        mix_specs, mix_ops = [_rows(tm, d), _resident(wo.shape)], [attn, wo]
    elif mixer == "pool":
        w_pool, scale = mix_args
        mix_specs = [_pool_halo_spec(tm, d), _resident(w_pool.shape), _resident(scale.shape)]
        mix_ops = [x, w_pool, scale]
    else:
        mix_specs, mix_ops = [], []
    return pl.pallas_call(
        functools.partial(_tail_kernel, mixer=mixer, alpha=alpha, seq_len=seq_len),
        grid=(tiles,),
        in_specs=[_rows(tm, d)] + mix_specs + [
            _resident(w_in.shape), _resident(w_out.shape),
            pl.BlockSpec((tm, p_all.shape[1]), lambda i: (layer * tiles + i, 0)),
            _resident(wg.shape), _resident(bg.shape), _resident(wp.shape),
            _resident(g3.shape), _resident(b3.shape)],
        out_specs=_rows(tm, d),
        out_shape=jax.ShapeDtypeStruct((rows, d), F32),
        scratch_shapes=[pltpu.VMEM((POOL_HALO + tm, d), F32)] if mixer == "pool" else [],
        compiler_params=_params(),
        name="tail_" + mixer,
    )(x, *mix_ops, w_in, w_out, p_all, wg, bg, wp, g3, b3)


def _log_sigmoid(z):
    return jnp.minimum(z, 0.0) - jnp.log1p(jnp.exp(-jnp.abs(z)))


def _lane_group_select(a, b, c):
    lane = lax.broadcasted_iota(jnp.int32, a.shape, 1)
    a, b, c = a.astype(F32), b.astype(F32), c.astype(F32)
    sel = jnp.where(lane < 16, a, jnp.where(lane < 32, b, jnp.where(
        lane < 48, c, jnp.where(lane < 64, 1.0, 0.0))))
    return sel.astype(BF16)


def _expand_heads(t):
    n = t.shape[1] // LANES
    return jnp.concatenate(
        [t[:, (gidx // 2) * LANES:(gidx // 2 + 1) * LANES] for gidx in range(2 * n)], axis=1)


def _fox_proj_prompt_kernel(x_ref, wt_ref, wf_ref, bf_ref, tri_ref, pk_ref, psel_ref, *rest,
                            q_scale, n_heads, first_layer):
    qxt_ref, kx_ref, vxt_ref, kt_ref, vt_ref, logft_ref, stats_ref, carry_sc = rest[-8:]

    @pl.when(pl.program_id(1) == 0)
    def _():
        carry_sc[...] = jnp.zeros_like(carry_sc)

    xb = x_ref[...].astype(BF16)
    tm, d = xb.shape
    ht = _dot_nt(wt_ref[...], xb)
    kt_ref[0, 0] = ht[d:2 * d]
    vt_ref[0, 0] = ht[2 * d:3 * d]
    if first_layer:
        for later in range(1, kt_ref.shape[0]):
            kt_ref[later, 0] = jnp.zeros((d, tm), F32)
            vt_ref[later, 0] = jnp.zeros((d, tm), F32)
    k_tok = ht[d:2 * d].T

    logf = _log_sigmoid(_dot(xb, wf_ref[...]) + bf_ref[...])
    logft_ref[0] = logf.T[0:n_heads]

    hi, mid, lo = _split3(logf)
    cs = _dot(tri_ref[...], jnp.concatenate([hi, mid, lo], axis=1))
    cum = (cs[:, :LANES] + cs[:, LANES:2 * LANES] + cs[:, 2 * LANES:]) + carry_sc[...]
    carry_sc[...] = cum[tm - 1:tm, :]

    f2 = cum * LOG2E
    fh, fm, fl = _split3(f2)
    ak = _dot(_lane_group_select(fh, fm, fl), pk_ref[...])
    lane = lax.broadcasted_iota(jnp.int32, ak.shape, 1)
    own = (((lane >> 6) ^ (lane >> 7)) & 1) == 0
    kx_ref[...] = jnp.where(own, _expand_heads(k_tok), ak).astype(BF16)

    f2t = f2.T[0:n_heads]
    th, tmid, tl = _split3(f2t)
    pieces = jnp.concatenate([th, tmid, tl, jnp.ones_like(th)], axis=0)
    aq = _dot(psel_ref[...], pieces).astype(BF16)
    qt = (ht[0:d] * q_scale).astype(BF16)
    vt = ht[2 * d:3 * d].astype(BF16)

    def head_max_norm(t):
        sq = (t * t).reshape(n_heads, HEAD_DIM, tm)
        return jnp.sqrt(jnp.max(jnp.sum(sq, axis=1), axis=1, keepdims=True))

    stats = [f2t[:, 0:1], f2t[:, tm - 1:tm], head_max_norm(qt.astype(F32)), head_max_norm(ht[d:2 * d])]
    stats_ref[0, 0] = jnp.concatenate(
        [jnp.broadcast_to(col, (n_heads, LANES)) for col in stats], axis=0)
    sub = lax.broadcasted_iota(jnp.int32, (16, tm), 0)
    ones_row = jnp.where(sub == 0, 1.0, 0.0).astype(BF16)
    zeros = jnp.zeros((HEAD_DIM - 16, tm), BF16)
    for h in range(n_heads):
        q_h = qt[h * HEAD_DIM:(h + 1) * HEAD_DIM]
        a_h = aq[16 * h:16 * (h + 1)]
        parts = [q_h, a_h, zeros] if h % 2 == 0 else [a_h, zeros, q_h]
        qxt_ref[0, 0, h * LANES:(h + 1) * LANES, :] = jnp.concatenate(parts, axis=0)
        vxt_ref[0, 0, h * V_ROWS:(h + 1) * V_ROWS, :] = jnp.concatenate(
            [vt[h * HEAD_DIM:(h + 1) * HEAD_DIM], ones_row], axis=0)


def _aug_placement(n_heads):
    pk = np.zeros((LANES, 2 * n_heads * HEAD_DIM), np.float32)
    psel = np.zeros((16 * n_heads, 4 * n_heads), np.float32)
    for h in range(n_heads):
        base = h * LANES + (HEAD_DIM if h % 2 == 0 else 0)
        for piece in range(3):
            pk[3 * n_heads + h, base + piece] = 1.0
            pk[n_heads * piece + h, base + 3 + piece] = -1.0
            psel[16 * h + piece, n_heads * piece + h] = 1.0
            psel[16 * h + 3 + piece, 3 * n_heads + h] = 1.0
    return jnp.asarray(pk, BF16), jnp.asarray(psel, BF16)


def _fox_proj_prompt(x, wt, wf, bf, batch, seq_len, q_scale, n_heads, layer, n_layers, kv_all):
    rows, d = x.shape
    tm = min(ATTN_TILE, seq_len)
    nt = seq_len // tm
    tri = jnp.asarray(np.tril(np.ones((tm, tm), np.float32)), BF16)
    pk, psel = _aug_placement(n_heads)
    fm_tiles = pl.BlockSpec((1, 1, 2 * d, tm), lambda b, t: (b, t, 0, 0))
    first = kv_all is None
    assert first == (layer == 0)
    kv_spec = pl.BlockSpec((n_layers if first else 1, 1, d, tm), lambda b, t: (layer, b, 0, t))
    kv_shape = jax.ShapeDtypeStruct((n_layers, batch, d, seq_len), F32)
    n_in = 7
    return pl.pallas_call(
        functools.partial(_fox_proj_prompt_kernel, q_scale=q_scale, n_heads=n_heads, first_layer=first),
        grid=(batch, nt),
        in_specs=[pl.BlockSpec((tm, d), lambda b, t: (b * nt + t, 0)), _resident(wt.shape),
                  _resident(wf.shape), _resident(bf.shape), _resident(tri.shape),
                  _resident(pk.shape), _resident(psel.shape)]
                 + ([] if first else [pl.BlockSpec(memory_space=pl.ANY)] * 2),
        out_specs=[fm_tiles, pl.BlockSpec((tm, 2 * d), lambda b, t: (b * nt + t, 0)),
                   pl.BlockSpec((1, 1, n_heads * V_ROWS, tm), lambda b, t: (b, t, 0, 0)),
                   kv_spec, kv_spec, pl.BlockSpec((1, n_heads, tm), lambda b, t: (b, 0, t)),
                   pl.BlockSpec((1, 1, 4 * n_heads, LANES), lambda b, t: (b, t, 0, 0))],
        out_shape=[jax.ShapeDtypeStruct((batch, nt, 2 * d, tm), BF16),
                   jax.ShapeDtypeStruct((rows, 2 * d), BF16),
                   jax.ShapeDtypeStruct((batch, nt, n_heads * V_ROWS, tm), BF16),
                   kv_shape, kv_shape,
                   jax.ShapeDtypeStruct((batch, n_heads, seq_len), F32),
                   jax.ShapeDtypeStruct((batch, nt, 4 * n_heads, LANES), F32)],
        input_output_aliases={} if first else {n_in: 3, n_in + 1: 4},
        scratch_shapes=[pltpu.VMEM((1, LANES), F32)],
        compiler_params=_params(2),
        name="fox_proj_prompt",
    )(x, wt, wf, bf, tri, pk, psel, *([] if first else kv_all))


def _fox_proj_sample_kernel(x_ref, wq_ref, wk_ref, wv_ref, wf_ref, bf_ref,
                            q_ref, k_ref, v_ref, logf_ref, *, q_scale, n_heads):
    xb = x_ref[...].astype(BF16)
    q_ref[...] = (_dot(xb, wq_ref[...]) * q_scale).astype(BF16)
    k_ref[...] = _dot(xb, wk_ref[...])
    v_ref[...] = _dot(xb, wv_ref[...])
    logf = _log_sigmoid(_dot(xb, wf_ref[...]) + bf_ref[...])
    logf_ref[...] = logf[:, :n_heads]


def _fox_proj_sample(x, wq, wk, wv, wf, bf, q_scale, n_heads):
    rows, d = x.shape
    tm = min(ROW_TILE, rows)
    return pl.pallas_call(
        functools.partial(_fox_proj_sample_kernel, q_scale=q_scale, n_heads=n_heads),
        grid=(rows // tm,),
        in_specs=[_rows(tm, d), _resident(wq.shape), _resident(wk.shape), _resident(wv.shape),
                  _resident(wf.shape), _resident(bf.shape)],
        out_specs=[_rows(tm, d), _rows(tm, d), _rows(tm, d), _rows(tm, n_heads)],
        out_shape=[jax.ShapeDtypeStruct((rows, d), BF16), jax.ShapeDtypeStruct((rows, d), F32),
                   jax.ShapeDtypeStruct((rows, d), F32), jax.ShapeDtypeStruct((rows, n_heads), F32)],
        compiler_params=_params(),
        name="fox_proj_sample",
    )(x, wq, wk, wv, wf, bf)


def _fox_attn_prompt_kernel(stats_ref, kx_ref, qxt_ref, vxt_ref, o_ref, *, tile, n_heads):
    n_blocks = qxt_ref.shape[1]
    key_in_block = lax.broadcasted_iota(jnp.int32, (tile, tile), 0)
    query_in_block = lax.broadcasted_iota(jnp.int32, (tile, tile), 1)
    causal = key_in_block <= query_in_block
    groups = [slice(h * LANES, (h + 1) * LANES) for h in range(2)]

    def scores(qi, kj, masked):
        k0 = pl.multiple_of(kj * tile, tile)
        out = []
        for h in range(2):
            s_t = _dot(kx_ref[pl.ds(k0, tile), groups[h]], qxt_ref[0, qi, groups[h], :])
            if masked:
                s_t = jnp.where(causal, s_t, -jnp.inf)
            out.append((s_t, jnp.max(s_t, axis=0, keepdims=True)))
        return tuple(out)

    def accumulate(kj, sc, state):
        new_state = []
        for h in range(2):
            m_prev, acc = state[h]
            s_t, m_blk = sc[h]
            m_new = jnp.maximum(m_prev, m_blk)
            alpha = jnp.exp2(m_prev - m_new)
            p_t = jnp.exp2(s_t - m_new).astype(BF16)
            acc = alpha * acc + _dot(vxt_ref[0, kj, h * V_ROWS:(h + 1) * V_ROWS, :], p_t)
            new_state.append((m_new, acc))
        return tuple(new_state)

    def finish(qi, state):
        o_t = jnp.concatenate(
            [acc[0:HEAD_DIM] / acc[HEAD_DIM:HEAD_DIM + 1] for _, acc in state], axis=0)
        o_ref[pl.ds(pl.multiple_of(qi * tile, tile), tile), :] = o_t.T.astype(o_ref.dtype)

    init = tuple((jnp.full((1, tile), NEG_BIG, F32), jnp.zeros((V_ROWS, tile), F32)) for _ in range(2))

    def merge(a, b):
        out = []
        for (m_a, acc_a), (m_b, acc_b) in zip(a, b):
            m = jnp.maximum(m_a, m_b)
            out.append((m, jnp.exp2(m_a - m) * acc_a + jnp.exp2(m_b - m) * acc_b))
        return tuple(out)

    batch_idx, pair_idx = pl.program_id(0), pl.program_id(1)

    def stat(kind, h, blk):
        return stats_ref[((batch_idx * 4 + kind) * n_heads + 2 * pair_idx + h) * n_blocks + blk]

    kn_max = [lax.fori_loop(0, n_blocks, lambda blk, mx: jnp.maximum(mx, stat(3, h, blk)), 0.0)
              for h in range(2)]

    def first_needed_block(qi):
        first = qi
        for h in range(2):
            slack = NORM_MARGIN * stat(2, h, qi) * (kn_max[h] + stat(3, h, qi)) + PRUNE_LOG2
            f_q = stat(0, h, qi)
            first = jnp.minimum(first, lax.while_loop(
                lambda j: jnp.logical_and(j < qi, f_q - stat(1, h, j) + slack <= 0.0),
                lambda j: j + 1, 0))
        return first

    def q_block(qi, carry):
        def pair(jp, both):
            sc_a = scores(qi, 2 * jp, False)
            sc_b = scores(qi, 2 * jp + 1, False)
            return accumulate(2 * jp, sc_a, both[0]), accumulate(2 * jp + 1, sc_b, both[1])

        st_a, st_b = lax.fori_loop(first_needed_block(qi) // 2, qi // 2, pair, (init, init))
        st_b = lax.fori_loop(
            0, qi % 2, lambda _, st: accumulate(qi - 1, scores(qi, qi - 1, False), st), st_b)
        st_a = accumulate(qi, scores(qi, qi, True), st_a)
        finish(qi, merge(st_a, st_b))
        return carry

    lax.fori_loop(0, n_blocks, q_block, 0)


def _fox_attn_prompt(stats, kx, qxt, vxt, batch, seq_len, n_heads):
    rows, d2 = kx.shape
    d = d2 // 2
    pairs = d // LANES
    tile = qxt.shape[3]
    n_blocks = seq_len // tile
    table = stats[:, :, :, 0].reshape(batch, n_blocks, 4, n_heads).transpose(0, 2, 3, 1).reshape(-1)
    feature_major = pl.BlockSpec((1, n_blocks, 2 * LANES, tile), lambda b, p: (b, 0, p, 0))
    return pl.pallas_call(
        functools.partial(_fox_attn_prompt_kernel, tile=tile, n_heads=n_heads),
        grid=(batch, pairs),
        in_specs=[pl.BlockSpec(memory_space=pltpu.SMEM),
                  pl.BlockSpec((seq_len, 2 * LANES), lambda b, p: (b, p)), feature_major,
                  pl.BlockSpec((1, n_blocks, 2 * V_ROWS, tile), lambda b, p: (b, 0, p, 0))],
        out_specs=pl.BlockSpec((seq_len, LANES), lambda b, p: (b, p)),
        out_shape=jax.ShapeDtypeStruct((rows, d), BF16),
        compiler_params=_params(2),
        name="fox_attn_prompt",
    )(table, kx, qxt, vxt)


def _fox_attn_sample_kernel(q_ref, kn_ref, vn_ref, ckt_ref, cvt_ref, lft_ref, o_ref, *, n_heads):
    t_new, d = q_ref.shape[1], q_ref.shape[2]
    past = ckt_ref.shape[3]
    keys = lft_ref.shape[2]
    qrows = n_heads * t_new

    cum = lft_ref[0]
    key_idx = lax.broadcasted_iota(jnp.int32, cum.shape, 1)
    step = 1
    while step < keys:
        cum = cum + jnp.where(key_idx >= step, pltpu.roll(cum, step, axis=1), 0.0)
        step *= 2

    r_query = lax.broadcasted_iota(jnp.int32, (qrows, t_new), 0) % t_new
    rep_query = jnp.where(r_query == lax.broadcasted_iota(jnp.int32, (qrows, t_new), 1),
                          1.0, 0.0).astype(BF16)

    cum2 = cum * LOG2E
    cum_rows = jnp.concatenate(
        [jnp.broadcast_to(cum2[h:h + 1, :], (t_new, keys)) for h in range(n_heads)], axis=0)
    kidx = lax.broadcasted_iota(jnp.int32, (qrows, keys), 1)
    qpos = past + lax.broadcasted_iota(jnp.int32, (qrows, keys), 0) % t_new
    cum_q = jnp.sum(jnp.where(kidx == qpos, cum_rows, 0.0), axis=1, keepdims=True)

    lane_head = lax.broadcasted_iota(jnp.int32, (qrows, d), 1) // HEAD_DIM
    row_head = lax.broadcasted_iota(jnp.int32, (qrows, d), 0) // t_new
    q_rep = _dot(rep_query, q_ref[0])
    q_exp = jnp.where(lane_head == row_head, q_rep, 0.0).astype(BF16)

    bias = cum_q - cum_rows
    s_old = _dot(q_exp, ckt_ref[0, 0].astype(BF16)) + bias[:, 0:past]
    s_new = _dot_nt(q_exp, kn_ref[0].astype(BF16)) + bias[:, past:past + t_new]
    new_key = lax.broadcasted_iota(jnp.int32, (qrows, t_new), 1)
    new_query = lax.broadcasted_iota(jnp.int32, (qrows, t_new), 0) % t_new
    s_new = jnp.where(new_key <= new_query, s_new, -jnp.inf)
    m = jnp.maximum(jnp.max(s_old, axis=1, keepdims=True), jnp.max(s_new, axis=1, keepdims=True))
    p_old = jnp.exp2(s_old - m)
    p_new = jnp.exp2(s_new - m)
    l = jnp.sum(p_old, axis=1, keepdims=True) + jnp.sum(p_new, axis=1, keepdims=True)
    o_all = (_dot_nt(p_old.astype(BF16), cvt_ref[0, 0].astype(BF16))
             + _dot(p_new.astype(BF16), vn_ref[0].astype(BF16))) / l
    o_all = jnp.where(lane_head == row_head, o_all, 0.0)
    out = o_all[0:t_new, :]
    for h in range(1, n_heads):
        out = out + o_all[h * t_new:(h + 1) * t_new, :]
    o_ref[0] = out.astype(o_ref.dtype)


def _fox_attn_sample(q, k_new, v_new, cache_kt, cache_vt, layer, lft, n_heads):
    nb, t_new, d = q.shape
    past = cache_kt.shape[3]
    keys = lft.shape[2]
    per_stream = lambda shape: pl.BlockSpec((1,) + shape, lambda b: (b, 0, 0))
    cache = pl.BlockSpec((1, 1, d, past), lambda b: (layer, b, 0, 0))
    return pl.pallas_call(
        functools.partial(_fox_attn_sample_kernel, n_heads=n_heads),
        grid=(nb,),
        in_specs=[per_stream((t_new, d)), per_stream((t_new, d)), per_stream((t_new, d)),
                  cache, cache, per_stream((n_heads, keys))],
        out_specs=per_stream((t_new, d)),
        out_shape=jax.ShapeDtypeStruct((nb, t_new, d), BF16),
        compiler_params=_params(),
        name="fox_attn_sample",
    )(q, k_new, v_new, cache_kt, cache_vt, lft)


def kernel(x_prompt, x_sample, cache_fox_k, cache_fox_v, cache_fox_logf, state_pool, p_prompt, p_sample,
           ln_g, ln_b, ffn_w_in, ffn_w_out, fox_w_in, fox_b_f, fox_w_o, pool_w, pool_scale,
           ple_w_proj, ple_w_gate, ple_b_gate):
    batch, seq, d = x_prompt.shape
    nb, t_new, _ = x_sample.shape
    depth = ln_g.shape[0]
    n_heads = d // HEAD_DIM
    d_ff = ffn_w_out.shape[2]
    past = cache_fox_k.shape[2]
    n_state = state_pool.shape[2]
    alpha = (2.0 * depth) ** 0.25
    q_scale = HEAD_DIM ** -0.5 * LOG2E
    assert d_ff % FFN_CHUNK == 0

    xp = x_prompt.reshape(batch * seq, d)
    xs = x_sample.reshape(nb * t_new, d)
    row = lambda a: a.reshape(1, -1)
    p_prompt_rows = p_prompt.reshape(depth * batch * seq, -1)
    p_sample_rows = p_sample.reshape(depth * nb * t_new, -1)
    cache_kt = jnp.transpose(cache_fox_k, (0, 1, 3, 4, 2)).reshape(-1, nb, d, past)
    cache_vt = jnp.transpose(cache_fox_v, (0, 1, 3, 4, 2)).reshape(-1, nb, d, past)

    n_fox = (depth + 1) // 2
    kv_all = None
    fp, poolp, ksm, vsm, fsm, pools = [], [], [], [], [], []
    for i in range(depth):
        j = i // 2

        def ffn_weights(s):
            return ffn_w_in[i, s].astype(BF16), ffn_w_out[i, s].astype(BF16)

        w_in1, w_out1 = ffn_weights(0)
        xp = _ffn_ln(xp, w_in1, w_out1, row(ln_g[i, 0]), row(ln_b[i, 0]), alpha)
        xs = _ffn_ln(xs, w_in1, w_out1, row(ln_g[i, 0]), row(ln_b[i, 0]), alpha)

        w_in2, w_out2 = ffn_weights(1)
        wg = ple_w_gate[i].astype(BF16)
        wp = ple_w_proj[i].astype(BF16)
        bg = row(ple_b_gate[i])
        g3, b3 = ln_g[i, 1:4], ln_b[i, 1:4]
        tail = functools.partial(_tail, w_in=w_in2, w_out=w_out2, layer=i, wg=wg, bg=bg, wp=wp,
                                 g3=g3, b3=b3, alpha=alpha)

        if i % 2 == 0:
            w = fox_w_in[j]
            wq, wk, wv = (w[:, c * d:(c + 1) * d].astype(BF16) for c in range(3))
            reps = LANES // n_heads
            wf = jnp.tile(w[:, 3 * d:], (1, reps)).astype(BF16)
            bf = jnp.tile(fox_b_f[j], reps).reshape(1, LANES)
            wo = fox_w_o[j].astype(BF16)

            wt = jnp.transpose(w)[:3 * d].astype(BF16)
            qxt, kx, vxt, kt_all, vt_all, ft, stats = _fox_proj_prompt(
                xp, wt, wf, bf, batch, seq, q_scale, n_heads, j, n_fox, kv_all)
            kv_all = (kt_all, vt_all)
            fp.append(ft)
            o = _fox_attn_prompt(stats, kx, qxt, vxt, batch, seq, n_heads)
            xp = tail(xp, "fox", (o, wo), p_all=p_prompt_rows)

            q2, k2, v2, f2 = _fox_proj_sample(xs, wq, wk, wv, wf, bf, q_scale, n_heads)
            keys = -(-(past + t_new) // LANES) * LANES
            lft = jnp.concatenate(
                [cache_fox_logf[j].astype(F32), f2.reshape(nb, t_new, n_heads),
                 jnp.zeros((nb, keys - past - t_new, n_heads), F32)], axis=1).transpose(0, 2, 1)
            o2 = _fox_attn_sample(q2.reshape(nb, t_new, d), k2.reshape(nb, t_new, d),
                                  v2.reshape(nb, t_new, d), cache_kt, cache_vt, j, lft, n_heads)
            xs = tail(xs, "fox", (o2.reshape(nb * t_new, d), wo), p_all=p_sample_rows)
            ksm.append(k2.reshape(nb, t_new, n_heads, HEAD_DIM))
            vsm.append(v2.reshape(nb, t_new, n_heads, HEAD_DIM))
            fsm.append(f2.reshape(nb, t_new, n_heads))
        else:
            wpool = pool_w[j].astype(BF16)
            sc = row(pool_scale[j])
            poolp.append(xp.reshape(batch, seq, d)[:, seq - n_state:])
            xp = tail(xp, "pool", (wpool, sc), p_all=p_prompt_rows, seq_len=seq)

            xs3 = xs.reshape(nb, t_new, d)
            ext = jnp.concatenate([state_pool[j], xs3.astype(state_pool.dtype)], axis=1)
            pools.append(ext[:, -n_state:])
            seg = POOL_HALO + t_new
            ext = jnp.concatenate([jnp.zeros((nb, seg - n_state - t_new, d), F32), ext], axis=1)
            ys = _pool_ln(ext.reshape(nb * seg, d), wpool, sc, row(g3[0]), row(b3[0]), alpha, seg)
            xs = ys.reshape(nb, seg, d)[:, seg - t_new:].reshape(nb * t_new, d)
            xs = tail(xs, "none", (), p_all=p_sample_rows)

    def token_major(all_layers):
        return jnp.transpose(all_layers.reshape(-1, batch, n_heads, HEAD_DIM, seq), (0, 1, 4, 2, 3))

    return (xp.reshape(batch, seq, d), xs.reshape(nb, t_new, d),
            token_major(kv_all[0]), token_major(kv_all[1]),
            jnp.transpose(jnp.stack(fp), (0, 1, 3, 2)), jnp.stack(poolp),
            jnp.stack(ksm), jnp.stack(vsm), jnp.stack(fsm), jnp.stack(pools))
```

```python
import functools

import numpy as np
import jax
import jax.numpy as jnp
from jax import lax
from jax.experimental import pallas as pl
from jax.experimental.pallas import tpu as pltpu

F32 = jnp.float32
BF16 = jnp.bfloat16

LANES = 128
HEAD_DIM = 64
POOL_WINDOWS = (2, 4, 8, 16)
POOL_HALO = 16
LN_EPS = 1e-5
LOG2E = 1.4426950408889634
NEG_BIG = -1e30
V_ROWS = HEAD_DIM + 16
PRUNE_LOG2 = 160.0
NORM_MARGIN = 1.02
VMEM_LIMIT = 56 * 1024 * 1024

ROW_TILE = 512
FFN_CHUNK = 256
ATTN_TILE = 512


def _params(n_axes=1):
    return pltpu.CompilerParams(
        dimension_semantics=("arbitrary",) * n_axes, vmem_limit_bytes=VMEM_LIMIT)


def _resident(shape):
    nd = len(shape)
    return pl.BlockSpec(shape, lambda *_: (0,) * nd, pipeline_mode=pl.Buffered(1))


def _rows(tm, width):
    return pl.BlockSpec((tm, width), lambda i: (i, 0))


def _sigmoid(z):
    return 1.0 / (1.0 + jnp.exp(-z))


def _post_norm(x, sub, g, b, alpha):
    y = alpha * x + sub
    mu = jnp.mean(y, axis=-1, keepdims=True)
    yc = y - mu
    var = jnp.mean(yc * yc, axis=-1, keepdims=True)
    return yc * lax.rsqrt(var + LN_EPS) * g + b


def _split3(x):
    hi = x.astype(BF16)
    r1 = x - hi.astype(F32)
    mid = r1.astype(BF16)
    lo = (r1 - mid.astype(F32)).astype(BF16)
    return hi, mid, lo


def _dot(a, b):
    return jnp.dot(a, b, preferred_element_type=F32)


def _dot_nt(a, b):
    return lax.dot_general(a, b, (((1,), (1,)), ((), ())), preferred_element_type=F32)


def _swiglu(x, win_ref, wout_ref):
    xb = x.astype(BF16)
    d_ff = wout_ref.shape[0]
    acc = None
    for c0 in range(0, d_ff, FFN_CHUNK):
        a = _dot(xb, win_ref[:, c0:c0 + FFN_CHUNK])
        u = _dot(xb, win_ref[:, d_ff + c0:d_ff + c0 + FFN_CHUNK])
        act = (a * _sigmoid(a) * u).astype(BF16)
        part = _dot(act, wout_ref[c0:c0 + FFN_CHUNK, :])
        acc = part if acc is None else acc + part
    return acc


def _ffn_ln_kernel(x_ref, win_ref, wout_ref, g_ref, b_ref, o_ref, *, alpha):
    x = x_ref[...]
    o_ref[...] = _post_norm(x, 0.5 * _swiglu(x, win_ref, wout_ref), g_ref[...], b_ref[...], alpha)


def _ffn_weight_specs(w_in_all, w_out_all, layer, slot):
    pick = lambda *_: (layer, slot, 0, 0)
    return [pl.BlockSpec((None, None) + w.shape[2:], pick, pipeline_mode=pl.Buffered(1))
            for w in (w_in_all, w_out_all)]


def _ffn_ln(x, w_in_all, w_out_all, layer, slot, g, b, alpha):
    rows, d = x.shape
    tm = min(ROW_TILE, rows)
    return pl.pallas_call(
        functools.partial(_ffn_ln_kernel, alpha=alpha),
        grid=(rows // tm,),
        in_specs=[_rows(tm, d)] + _ffn_weight_specs(w_in_all, w_out_all, layer, slot) + [
            _resident(g.shape), _resident(b.shape)],
        out_specs=_rows(tm, d),
        out_shape=jax.ShapeDtypeStruct((rows, d), F32),
        compiler_params=_params(),
        name="ffn_ln",
    )(x, w_in_all, w_out_all, g, b)


def _pool_mix(x, halo_ref, w_ref, scale_ref, ext_sc, seq_len):
    tm, d = x.shape
    start = (pl.program_id(0) * tm) % seq_len
    grp = d // len(POOL_WINDOWS)
    ext_sc[0:POOL_HALO, :] = jnp.where(start == 0, 0.0, halo_ref[...])
    ext_sc[POOL_HALO:, :] = x
    pos = (start + lax.broadcasted_iota(jnp.int32, (tm, 1), 0)) % seq_len
    outs = []
    for gi, w in enumerate(POOL_WINDOWS):
        sl = slice(gi * grp, (gi + 1) * grp)
        tot = x[:, sl]
        for back in range(1, w):
            tot = tot + ext_sc[POOL_HALO - back:POOL_HALO - back + tm, sl]
        cnt = jnp.minimum(pos + 1, w).astype(F32)
        pooled = tot / cnt - x[:, sl]
        outs.append(_dot(pooled.astype(BF16), w_ref[gi]))
    return jnp.concatenate(outs, axis=1) * scale_ref[...]


def _pool_halo_spec(tm, d):
    halo_blocks = tm // POOL_HALO
    return pl.BlockSpec((POOL_HALO, d), lambda i: (jnp.maximum(i * halo_blocks - 1, 0), 0))


def _pool_ln_kernel(x_ref, halo_ref, w_ref, scale_ref, g_ref, b_ref, o_ref, ext_sc, *, alpha, seq_len):
    x = x_ref[...]
    y = _pool_mix(x, halo_ref, w_ref, scale_ref, ext_sc, seq_len)
    o_ref[...] = _post_norm(x, y, g_ref[...], b_ref[...], alpha)


def _pool_ln(x, w_pool, scale, g, b, alpha, seq_len):
    rows, d = x.shape
    tm = min(ROW_TILE, rows)
    assert seq_len % tm == 0 or tm % seq_len == 0
    return pl.pallas_call(
        functools.partial(_pool_ln_kernel, alpha=alpha, seq_len=seq_len),
        grid=(rows // tm,),
        in_specs=[_rows(tm, d), _pool_halo_spec(tm, d), _resident(w_pool.shape), _resident(scale.shape),
                  _resident(g.shape), _resident(b.shape)],
        out_specs=_rows(tm, d),
        out_shape=jax.ShapeDtypeStruct((rows, d), F32),
        scratch_shapes=[pltpu.VMEM((POOL_HALO + tm, d), F32)],
        compiler_params=_params(),
        name="pool_ln",
    )(x, x, w_pool, scale, g, b)


def _tail_kernel(*refs, mixer, alpha, seq_len):
    x_ref, refs = refs[0], refs[1:]
    n_mix = {"fox": 2, "pool": 3, "none": 0}[mixer]
    mix_refs, refs = refs[:n_mix], refs[n_mix:]
    win_ref, wout_ref, p_ref, wg_ref, bg_ref, wp_ref, g_ref, b_ref, o_ref = refs[:9]
    x = x_ref[...]
    if mixer == "fox":
        attn_ref, wo_ref = mix_refs
        x = _post_norm(x, _dot(attn_ref[...], wo_ref[...]), g_ref[0:1], b_ref[0:1], alpha)
    elif mixer == "pool":
        halo_ref, wpool_ref, scale_ref = mix_refs
        y = _pool_mix(x, halo_ref, wpool_ref, scale_ref, refs[9], seq_len)
        x = _post_norm(x, y, g_ref[0:1], b_ref[0:1], alpha)
    x = _post_norm(x, 0.5 * _swiglu(x, win_ref, wout_ref), g_ref[1:2], b_ref[1:2], alpha)
    gate = _sigmoid(_dot(x.astype(BF16), wg_ref[...]) + bg_ref[...])
    proj = _dot(p_ref[...].astype(BF16), wp_ref[...])
    o_ref[...] = _post_norm(x, proj * gate, g_ref[2:3], b_ref[2:3], alpha)


def _tail(x, mixer, mix_args, w_in_all, w_out_all, p_all, layer, wg, bg, wp, g3, b3, alpha, seq_len=None):
    rows, d = x.shape
    tm = min(ROW_TILE, rows)
    tiles = rows // tm
    if mixer == "fox":
        attn, wo = mix_args
        mix_specs, mix_ops = [_rows(tm, d), _resident(wo.shape)], [attn, wo]
    elif mixer == "pool":
        w_pool, scale = mix_args
        mix_specs = [_pool_halo_spec(tm, d), _resident(w_pool.shape), _resident(scale.shape)]
        mix_ops = [x, w_pool, scale]
    else:
        mix_specs, mix_ops = [], []
    return pl.pallas_call(
        functools.partial(_tail_kernel, mixer=mixer, alpha=alpha, seq_len=seq_len),
        grid=(tiles,),
        in_specs=[_rows(tm, d)] + mix_specs + _ffn_weight_specs(w_in_all, w_out_all, layer, 1) + [
            pl.BlockSpec((tm, p_all.shape[1]), lambda i: (layer * tiles + i, 0)),
            _resident(wg.shape), _resident(bg.shape), _resident(wp.shape),
            _resident(g3.shape), _resident(b3.shape)],
        out_specs=_rows(tm, d),
        out_shape=jax.ShapeDtypeStruct((rows, d), F32),
        scratch_shapes=[pltpu.VMEM((POOL_HALO + tm, d), F32)] if mixer == "pool" else [],
        compiler_params=_params(),
        name="tail_" + mixer,
    )(x, *mix_ops, w_in_all, w_out_all, p_all, wg, bg, wp, g3, b3)


def _log_sigmoid(z):
    return jnp.minimum(z, 0.0) - jnp.log1p(jnp.exp(-jnp.abs(z)))


def _lane_group_select(a, b, c):
    lane = lax.broadcasted_iota(jnp.int32, a.shape, 1)
    a, b, c = a.astype(F32), b.astype(F32), c.astype(F32)
    sel = jnp.where(lane < 16, a, jnp.where(lane < 32, b, jnp.where(
        lane < 48, c, jnp.where(lane < 64, 1.0, 0.0))))
    return sel.astype(BF16)


def _expand_heads(t):
    n = t.shape[1] // LANES
    return jnp.concatenate(
        [t[:, (gidx // 2) * LANES:(gidx // 2 + 1) * LANES] for gidx in range(2 * n)], axis=1)


def _fox_proj_prompt_kernel(x_ref, wt_ref, wf_ref, bf_ref, tri_ref, pk_ref, psel_ref, *rest,
                            q_scale, n_heads, first_layer):
    qxt_ref, kx_ref, vxt_ref, kt_ref, vt_ref, logft_ref, stats_ref, carry_sc = rest[-8:]

    @pl.when(pl.program_id(1) == 0)
    def _():
        carry_sc[...] = jnp.zeros_like(carry_sc)

    xb = x_ref[...].astype(BF16)
    tm, d = xb.shape
    ht = _dot_nt(wt_ref[...], xb)
    kt_ref[0, 0] = ht[d:2 * d]
    vt_ref[0, 0] = ht[2 * d:3 * d]
    if first_layer:
        for later in range(1, kt_ref.shape[0]):
            kt_ref[later, 0] = jnp.zeros((d, tm), F32)
            vt_ref[later, 0] = jnp.zeros((d, tm), F32)
    k_tok = ht[d:2 * d].T

    logf = _log_sigmoid(_dot(xb, wf_ref[...]) + bf_ref[...])
    logft_ref[0] = logf.T[0:n_heads]

    hi, mid, lo = _split3(logf)
    cs = _dot(tri_ref[...], jnp.concatenate([hi, mid, lo], axis=1))
    cum = (cs[:, :LANES] + cs[:, LANES:2 * LANES] + cs[:, 2 * LANES:]) + carry_sc[...]
    carry_sc[...] = cum[tm - 1:tm, :]

    f2 = cum * LOG2E
    fh, fm, fl = _split3(f2)
    ak = _dot(_lane_group_select(fh, fm, fl), pk_ref[...])
    lane = lax.broadcasted_iota(jnp.int32, ak.shape, 1)
    own = (((lane >> 6) ^ (lane >> 7)) & 1) == 0
    kx_ref[...] = jnp.where(own, _expand_heads(k_tok), ak).astype(BF16)

    f2t = f2.T[0:n_heads]
    th, tmid, tl = _split3(f2t)
    pieces = jnp.concatenate([th, tmid, tl, jnp.ones_like(th)], axis=0)
    aq = _dot(psel_ref[...], pieces).astype(BF16)
    qt = (ht[0:d] * q_scale).astype(BF16)
    vt = ht[2 * d:3 * d].astype(BF16)

    def head_max_norm(t):
        sq = (t * t).reshape(n_heads, HEAD_DIM, tm)
        return jnp.sqrt(jnp.max(jnp.sum(sq, axis=1), axis=1, keepdims=True))

    stats = [f2t[:, 0:1], f2t[:, tm - 1:tm], head_max_norm(qt.astype(F32)), head_max_norm(ht[d:2 * d])]
    stats_ref[0, 0] = jnp.concatenate(
        [jnp.broadcast_to(col, (n_heads, LANES)) for col in stats], axis=0)
    sub = lax.broadcasted_iota(jnp.int32, (16, tm), 0)
    ones_row = jnp.where(sub == 0, 1.0, 0.0).astype(BF16)
    zeros = jnp.zeros((HEAD_DIM - 16, tm), BF16)
    for h in range(n_heads):
        q_h = qt[h * HEAD_DIM:(h + 1) * HEAD_DIM]
        a_h = aq[16 * h:16 * (h + 1)]
        parts = [q_h, a_h, zeros] if h % 2 == 0 else [a_h, zeros, q_h]
        qxt_ref[0, 0, h * LANES:(h + 1) * LANES, :] = jnp.concatenate(parts, axis=0)
        vxt_ref[0, 0, h * V_ROWS:(h + 1) * V_ROWS, :] = jnp.concatenate(
            [vt[h * HEAD_DIM:(h + 1) * HEAD_DIM], ones_row], axis=0)


def _aug_placement(n_heads):
    pk = np.zeros((LANES, 2 * n_heads * HEAD_DIM), np.float32)
    psel = np.zeros((16 * n_heads, 4 * n_heads), np.float32)
    for h in range(n_heads):
        base = h * LANES + (HEAD_DIM if h % 2 == 0 else 0)
        for piece in range(3):
            pk[3 * n_heads + h, base + piece] = 1.0
            pk[n_heads * piece + h, base + 3 + piece] = -1.0
            psel[16 * h + piece, n_heads * piece + h] = 1.0
            psel[16 * h + 3 + piece, 3 * n_heads + h] = 1.0
    return jnp.asarray(pk, BF16), jnp.asarray(psel, BF16)


def _fox_proj_prompt(x, wt, wf, bf, batch, seq_len, q_scale, n_heads, layer, n_layers, kv_all):
    rows, d = x.shape
    tm = min(ATTN_TILE, seq_len)
    nt = seq_len // tm
    tri = jnp.asarray(np.tril(np.ones((tm, tm), np.float32)), BF16)
    pk, psel = _aug_placement(n_heads)
    fm_tiles = pl.BlockSpec((1, 1, 2 * d, tm), lambda b, t: (b, t, 0, 0))
    first = kv_all is None
    assert first == (layer == 0)
    kv_spec = pl.BlockSpec((n_layers if first else 1, 1, d, tm), lambda b, t: (layer, b, 0, t))
    kv_shape = jax.ShapeDtypeStruct((n_layers, batch, d, seq_len), F32)
    n_in = 7
    return pl.pallas_call(
        functools.partial(_fox_proj_prompt_kernel, q_scale=q_scale, n_heads=n_heads, first_layer=first),
        grid=(batch, nt),
        in_specs=[pl.BlockSpec((tm, d), lambda b, t: (b * nt + t, 0)), _resident(wt.shape),
                  _resident(wf.shape), _resident(bf.shape), _resident(tri.shape),
                  _resident(pk.shape), _resident(psel.shape)]
                 + ([] if first else [pl.BlockSpec(memory_space=pl.ANY)] * 2),
        out_specs=[fm_tiles, pl.BlockSpec((tm, 2 * d), lambda b, t: (b * nt + t, 0)),
                   pl.BlockSpec((1, 1, n_heads * V_ROWS, tm), lambda b, t: (b, t, 0, 0)),
                   kv_spec, kv_spec, pl.BlockSpec((1, n_heads, tm), lambda b, t: (b, 0, t)),
                   pl.BlockSpec((1, 1, 4 * n_heads, LANES), lambda b, t: (b, t, 0, 0))],
        out_shape=[jax.ShapeDtypeStruct((batch, nt, 2 * d, tm), BF16),
                   jax.ShapeDtypeStruct((rows, 2 * d), BF16),
                   jax.ShapeDtypeStruct((batch, nt, n_heads * V_ROWS, tm), BF16),
                   kv_shape, kv_shape,
                   jax.ShapeDtypeStruct((batch, n_heads, seq_len), F32),
                   jax.ShapeDtypeStruct((batch, nt, 4 * n_heads, LANES), F32)],
        input_output_aliases={} if first else {n_in: 3, n_in + 1: 4},
        scratch_shapes=[pltpu.VMEM((1, LANES), F32)],
        compiler_params=_params(2),
        name="fox_proj_prompt",
    )(x, wt, wf, bf, tri, pk, psel, *([] if first else kv_all))


def _fox_proj_sample_kernel(x_ref, wq_ref, wk_ref, wv_ref, wf_ref, bf_ref,
                            q_ref, k_ref, v_ref, logf_ref, *, q_scale, n_heads):
    xb = x_ref[...].astype(BF16)
    q_ref[...] = (_dot(xb, wq_ref[...]) * q_scale).astype(BF16)
    k_ref[...] = _dot(xb, wk_ref[...])
    v_ref[...] = _dot(xb, wv_ref[...])
    logf = _log_sigmoid(_dot(xb, wf_ref[...]) + bf_ref[...])
    logf_ref[...] = logf[:, :n_heads]


def _fox_proj_sample(x, wq, wk, wv, wf, bf, q_scale, n_heads):
    rows, d = x.shape
    tm = min(ROW_TILE, rows)
    return pl.pallas_call(
        functools.partial(_fox_proj_sample_kernel, q_scale=q_scale, n_heads=n_heads),
        grid=(rows // tm,),
        in_specs=[_rows(tm, d), _resident(wq.shape), _resident(wk.shape), _resident(wv.shape),
                  _resident(wf.shape), _resident(bf.shape)],
        out_specs=[_rows(tm, d), _rows(tm, d), _rows(tm, d), _rows(tm, n_heads)],
        out_shape=[jax.ShapeDtypeStruct((rows, d), BF16), jax.ShapeDtypeStruct((rows, d), F32),
                   jax.ShapeDtypeStruct((rows, d), F32), jax.ShapeDtypeStruct((rows, n_heads), F32)],
        compiler_params=_params(),
        name="fox_proj_sample",
    )(x, wq, wk, wv, wf, bf)


def _fox_attn_prompt_kernel(stats_ref, kx_ref, qxt_ref, vxt_ref, o_ref, *, tile, n_heads):
    n_blocks = qxt_ref.shape[1]
    key_in_block = lax.broadcasted_iota(jnp.int32, (tile, tile), 0)
    query_in_block = lax.broadcasted_iota(jnp.int32, (tile, tile), 1)
    causal = key_in_block <= query_in_block
    groups = [slice(h * LANES, (h + 1) * LANES) for h in range(2)]

    def scores(qi, kj, masked):
        k0 = pl.multiple_of(kj * tile, tile)
        out = []
        for h in range(2):
            s_t = _dot(kx_ref[pl.ds(k0, tile), groups[h]], qxt_ref[0, qi, groups[h], :])
            if masked:
                s_t = jnp.where(causal, s_t, -jnp.inf)
            out.append((s_t, jnp.max(s_t, axis=0, keepdims=True)))
        return tuple(out)

    def accumulate(kj, sc, state):
        new_state = []
        for h in range(2):
            m_prev, acc = state[h]
            s_t, m_blk = sc[h]
            m_new = jnp.maximum(m_prev, m_blk)
            alpha = jnp.exp2(m_prev - m_new)
            p_t = jnp.exp2(s_t - m_new).astype(BF16)
            acc = alpha * acc + _dot(vxt_ref[0, kj, h * V_ROWS:(h + 1) * V_ROWS, :], p_t)
            new_state.append((m_new, acc))
        return tuple(new_state)

    def finish(qi, state):
        o_t = jnp.concatenate(
            [acc[0:HEAD_DIM] / acc[HEAD_DIM:HEAD_DIM + 1] for _, acc in state], axis=0)
        o_ref[pl.ds(pl.multiple_of(qi * tile, tile), tile), :] = o_t.T.astype(o_ref.dtype)

    init = tuple((jnp.full((1, tile), NEG_BIG, F32), jnp.zeros((V_ROWS, tile), F32)) for _ in range(2))

    def merge(a, b):
        out = []
        for (m_a, acc_a), (m_b, acc_b) in zip(a, b):
            m = jnp.maximum(m_a, m_b)
            out.append((m, jnp.exp2(m_a - m) * acc_a + jnp.exp2(m_b - m) * acc_b))
        return tuple(out)

    batch_idx, pair_idx = pl.program_id(0), pl.program_id(1)

    def stat(kind, h, blk):
        return stats_ref[((batch_idx * 4 + kind) * n_heads + 2 * pair_idx + h) * n_blocks + blk]

    kn_max = [lax.fori_loop(0, n_blocks, lambda blk, mx: jnp.maximum(mx, stat(3, h, blk)), 0.0)
              for h in range(2)]

    def first_needed_block(qi):
        first = qi
        for h in range(2):
            slack = NORM_MARGIN * stat(2, h, qi) * (kn_max[h] + stat(3, h, qi)) + PRUNE_LOG2
            f_q = stat(0, h, qi)
            first = jnp.minimum(first, lax.while_loop(
                lambda j: jnp.logical_and(j < qi, f_q - stat(1, h, j) + slack <= 0.0),
                lambda j: j + 1, 0))
        return first

    def q_block(qi, carry):
        def pair(jp, both):
            sc_a = scores(qi, 2 * jp, False)
            sc_b = scores(qi, 2 * jp + 1, False)
            return accumulate(2 * jp, sc_a, both[0]), accumulate(2 * jp + 1, sc_b, both[1])

        st_a, st_b = lax.fori_loop(first_needed_block(qi) // 2, qi // 2, pair, (init, init))
        st_b = lax.fori_loop(
            0, qi % 2, lambda _, st: accumulate(qi - 1, scores(qi, qi - 1, False), st), st_b)
        st_a = accumulate(qi, scores(qi, qi, True), st_a)
        finish(qi, merge(st_a, st_b))
        return carry

    lax.fori_loop(0, n_blocks, q_block, 0)


def _fox_attn_prompt(stats, kx, qxt, vxt, batch, seq_len, n_heads):
    rows, d2 = kx.shape
    d = d2 // 2
    pairs = d // LANES
    tile = qxt.shape[3]
    n_blocks = seq_len // tile
    table = stats[:, :, :, 0].reshape(batch, n_blocks, 4, n_heads).transpose(0, 2, 3, 1).reshape(-1)
    feature_major = pl.BlockSpec((1, n_blocks, 2 * LANES, tile), lambda b, p: (b, 0, p, 0))
    return pl.pallas_call(
        functools.partial(_fox_attn_prompt_kernel, tile=tile, n_heads=n_heads),
        grid=(batch, pairs),
        in_specs=[pl.BlockSpec(memory_space=pltpu.SMEM),
                  pl.BlockSpec((seq_len, 2 * LANES), lambda b, p: (b, p)), feature_major,
                  pl.BlockSpec((1, n_blocks, 2 * V_ROWS, tile), lambda b, p: (b, 0, p, 0))],
        out_specs=pl.BlockSpec((seq_len, LANES), lambda b, p: (b, p)),
        out_shape=jax.ShapeDtypeStruct((rows, d), BF16),
        compiler_params=_params(2),
        name="fox_attn_prompt",
    )(table, kx, qxt, vxt)


def _fox_attn_sample_kernel(q_ref, kn_ref, vn_ref, ckt_ref, cvt_ref, lft_ref, o_ref, *, n_heads):
    t_new, d = q_ref.shape[1], q_ref.shape[2]
    past = ckt_ref.shape[3]
    keys = lft_ref.shape[2]
    qrows = n_heads * t_new

    cum = lft_ref[0]
    key_idx = lax.broadcasted_iota(jnp.int32, cum.shape, 1)
    step = 1
    while step < keys:
        cum = cum + jnp.where(key_idx >= step, pltpu.roll(cum, step, axis=1), 0.0)
        step *= 2

    r_query = lax.broadcasted_iota(jnp.int32, (qrows, t_new), 0) % t_new
    rep_query = jnp.where(r_query == lax.broadcasted_iota(jnp.int32, (qrows, t_new), 1),
                          1.0, 0.0).astype(BF16)

    cum2 = cum * LOG2E
    cum_rows = jnp.concatenate(
        [jnp.broadcast_to(cum2[h:h + 1, :], (t_new, keys)) for h in range(n_heads)], axis=0)
    kidx = lax.broadcasted_iota(jnp.int32, (qrows, keys), 1)
    qpos = past + lax.broadcasted_iota(jnp.int32, (qrows, keys), 0) % t_new
    cum_q = jnp.sum(jnp.where(kidx == qpos, cum_rows, 0.0), axis=1, keepdims=True)

    lane_head = lax.broadcasted_iota(jnp.int32, (qrows, d), 1) // HEAD_DIM
    row_head = lax.broadcasted_iota(jnp.int32, (qrows, d), 0) // t_new
    q_rep = _dot(rep_query, q_ref[0])
    q_exp = jnp.where(lane_head == row_head, q_rep, 0.0).astype(BF16)

    bias = cum_q - cum_rows
    s_old = _dot(q_exp, ckt_ref[0, 0].astype(BF16)) + bias[:, 0:past]
    s_new = _dot_nt(q_exp, kn_ref[0].astype(BF16)) + bias[:, past:past + t_new]
    new_key = lax.broadcasted_iota(jnp.int32, (qrows, t_new), 1)
    new_query = lax.broadcasted_iota(jnp.int32, (qrows, t_new), 0) % t_new
    s_new = jnp.where(new_key <= new_query, s_new, -jnp.inf)
    m = jnp.maximum(jnp.max(s_old, axis=1, keepdims=True), jnp.max(s_new, axis=1, keepdims=True))
    p_old = jnp.exp2(s_old - m)
    p_new = jnp.exp2(s_new - m)
    l = jnp.sum(p_old, axis=1, keepdims=True) + jnp.sum(p_new, axis=1, keepdims=True)
    o_all = (_dot_nt(p_old.astype(BF16), cvt_ref[0, 0].astype(BF16))
             + _dot(p_new.astype(BF16), vn_ref[0].astype(BF16))) / l
    o_all = jnp.where(lane_head == row_head, o_all, 0.0)
    out = o_all[0:t_new, :]
    for h in range(1, n_heads):
        out = out + o_all[h * t_new:(h + 1) * t_new, :]
    o_ref[0] = out.astype(o_ref.dtype)


def _fox_attn_sample(q, k_new, v_new, cache_kt, cache_vt, layer, lft, n_heads):
    nb, t_new, d = q.shape
    past = cache_kt.shape[3]
    keys = lft.shape[2]
    per_stream = lambda shape: pl.BlockSpec((1,) + shape, lambda b: (b, 0, 0))
    cache = pl.BlockSpec((1, 1, d, past), lambda b: (layer, b, 0, 0))
    return pl.pallas_call(
        functools.partial(_fox_attn_sample_kernel, n_heads=n_heads),
        grid=(nb,),
        in_specs=[per_stream((t_new, d)), per_stream((t_new, d)), per_stream((t_new, d)),
                  cache, cache, per_stream((n_heads, keys))],
        out_specs=per_stream((t_new, d)),
        out_shape=jax.ShapeDtypeStruct((nb, t_new, d), BF16),
        compiler_params=_params(),
        name="fox_attn_sample",
    )(q, k_new, v_new, cache_kt, cache_vt, lft)


def kernel(x_prompt, x_sample, cache_fox_k, cache_fox_v, cache_fox_logf, state_pool, p_prompt, p_sample,
           ln_g, ln_b, ffn_w_in, ffn_w_out, fox_w_in, fox_b_f, fox_w_o, pool_w, pool_scale,
           ple_w_proj, ple_w_gate, ple_b_gate):
    batch, seq, d = x_prompt.shape
    nb, t_new, _ = x_sample.shape
    depth = ln_g.shape[0]
    n_heads = d // HEAD_DIM
    d_ff = ffn_w_out.shape[2]
    past = cache_fox_k.shape[2]
    n_state = state_pool.shape[2]
    alpha = (2.0 * depth) ** 0.25
    q_scale = HEAD_DIM ** -0.5 * LOG2E
    assert d_ff % FFN_CHUNK == 0

    xp = x_prompt.reshape(batch * seq, d)
    xs = x_sample.reshape(nb * t_new, d)
    row = lambda a: a.reshape(1, -1)
    p_prompt_rows = p_prompt.reshape(depth * batch * seq, -1)
    p_sample_rows = p_sample.reshape(depth * nb * t_new, -1)
    cache_kt = jnp.transpose(cache_fox_k, (0, 1, 3, 4, 2)).reshape(-1, nb, d, past)
    cache_vt = jnp.transpose(cache_fox_v, (0, 1, 3, 4, 2)).reshape(-1, nb, d, past)

    w_in_all = ffn_w_in.astype(BF16)
    w_out_all = ffn_w_out.astype(BF16)

    n_fox = (depth + 1) // 2
    kv_all = None
    fp, poolp, ksm, vsm, fsm, pools = [], [], [], [], [], []
    for i in range(depth):
        j = i // 2

        xp = _ffn_ln(xp, w_in_all, w_out_all, i, 0, row(ln_g[i, 0]), row(ln_b[i, 0]), alpha)
        xs = _ffn_ln(xs, w_in_all, w_out_all, i, 0, row(ln_g[i, 0]), row(ln_b[i, 0]), alpha)

        wg = ple_w_gate[i].astype(BF16)
        wp = ple_w_proj[i].astype(BF16)
        bg = row(ple_b_gate[i])
        g3, b3 = ln_g[i, 1:4], ln_b[i, 1:4]
        tail = functools.partial(_tail, w_in_all=w_in_all, w_out_all=w_out_all, layer=i, wg=wg, bg=bg,
                                 wp=wp, g3=g3, b3=b3, alpha=alpha)

        if i % 2 == 0:
            w = fox_w_in[j]
            wq, wk, wv = (w[:, c * d:(c + 1) * d].astype(BF16) for c in range(3))
            reps = LANES // n_heads
            wf = jnp.tile(w[:, 3 * d:], (1, reps)).astype(BF16)
            bf = jnp.tile(fox_b_f[j], reps).reshape(1, LANES)
            wo = fox_w_o[j].astype(BF16)

            wt = jnp.transpose(w)[:3 * d].astype(BF16)
            qxt, kx, vxt, kt_all, vt_all, ft, stats = _fox_proj_prompt(
                xp, wt, wf, bf, batch, seq, q_scale, n_heads, j, n_fox, kv_all)
            kv_all = (kt_all, vt_all)
            fp.append(ft)
            o = _fox_attn_prompt(stats, kx, qxt, vxt, batch, seq, n_heads)
            xp = tail(xp, "fox", (o, wo), p_all=p_prompt_rows)

            q2, k2, v2, f2 = _fox_proj_sample(xs, wq, wk, wv, wf, bf, q_scale, n_heads)
            keys = -(-(past + t_new) // LANES) * LANES
            lft = jnp.concatenate(
                [cache_fox_logf[j].astype(F32), f2.reshape(nb, t_new, n_heads),
                 jnp.zeros((nb, keys - past - t_new, n_heads), F32)], axis=1).transpose(0, 2, 1)
            o2 = _fox_attn_sample(q2.reshape(nb, t_new, d), k2.reshape(nb, t_new, d),
                                  v2.reshape(nb, t_new, d), cache_kt, cache_vt, j, lft, n_heads)
            xs = tail(xs, "fox", (o2.reshape(nb * t_new, d), wo), p_all=p_sample_rows)
            ksm.append(k2.reshape(nb, t_new, n_heads, HEAD_DIM))
            vsm.append(v2.reshape(nb, t_new, n_heads, HEAD_DIM))
            fsm.append(f2.reshape(nb, t_new, n_heads))
        else:
            wpool = pool_w[j].astype(BF16)
            sc = row(pool_scale[j])
            poolp.append(xp.reshape(batch, seq, d)[:, seq - n_state:])
            xp = tail(xp, "pool", (wpool, sc), p_all=p_prompt_rows, seq_len=seq)

            xs3 = xs.reshape(nb, t_new, d)
            ext = jnp.concatenate([state_pool[j], xs3.astype(state_pool.dtype)], axis=1)
            pools.append(ext[:, -n_state:])
            seg = POOL_HALO + t_new
            ext = jnp.concatenate([jnp.zeros((nb, seg - n_state - t_new, d), F32), ext], axis=1)
            ys = _pool_ln(ext.reshape(nb * seg, d), wpool, sc, row(g3[0]), row(b3[0]), alpha, seg)
            xs = ys.reshape(nb, seg, d)[:, seg - t_new:].reshape(nb * t_new, d)
            xs = tail(xs, "none", (), p_all=p_sample_rows)

    def token_major(all_layers):
        return jnp.transpose(all_layers.reshape(-1, batch, n_heads, HEAD_DIM, seq), (0, 1, 4, 2, 3))

    return (xp.reshape(batch, seq, d), xs.reshape(nb, t_new, d),
            token_major(kv_all[0]), token_major(kv_all[1]),
            jnp.transpose(jnp.stack(fp), (0, 1, 3, 2)), jnp.stack(poolp),
            jnp.stack(ksm), jnp.stack(vsm), jnp.stack(fsm), jnp.stack(pools))
```
